```python
import math
import jax
import jax.numpy as jnp
from jax import lax
import numpy as np

D_MODEL = 2048
BATCH = 4
SEQ = 4096
DEPTH = 2
DEC_BATCH = 4
DEC_SEQ = 8192
PAST_LEN = 128

EPS = 1e-6
D_MIX = D_MODEL
SSD_HEADS = 16
SSD_HEAD_DIM = 64
D_SSD = SSD_HEADS * SSD_HEAD_DIM
SSD_GROUPS = 2
SSD_STATE = 128
SSD_CONV = 5
SSD_CHUNK = 128
D_XBC = D_SSD + 2 * SSD_GROUPS * SSD_STATE
S5_GROUP = 16
S5_GROUPS = 32
D_S5 = S5_GROUP * S5_GROUPS
S5_STATE = 64
NA_HEADS = 8
NA_HEAD_DIM = 64
D_NA = NA_HEADS * NA_HEAD_DIM
GRID_W = 64
NA_KH_MAX = 8
NA_KW = 16
NA_QB = 16
NA_KB = 32
N_EXPERTS = 16
EXPERT_FF = 2048
EC_CAPACITY = 2
D_IN_PROJ = D_SSD + D_XBC + 2 * SSD_HEADS + D_S5 + 3 * D_NA

kernel_name = "hybrid_ssd_s5_natten_ec_encoder"


def rms_norm(x, g):
    xf = x.astype(jnp.float32)
    y = xf * lax.rsqrt(jnp.mean(xf * xf, axis=-1, keepdims=True) + EPS)
    return (y * g.astype(jnp.float32)).astype(x.dtype)


def centred_depthwise_conv(x, w, b):
    k = w.shape[0]
    left = k // 2
    y = lax.conv_general_dilated(
        x, w.astype(x.dtype)[:, None, :], window_strides=(1,), padding=[(left, k - 1 - left)],
        dimension_numbers=('NWC', 'WIO', 'NWC'), feature_group_count=x.shape[-1])
    return y + b.astype(x.dtype)


def ssd_chunked_scan(x, dt, a, bm, cm):
    bsz, seqlen = x.shape[0], x.shape[1]
    nc, q = seqlen // SSD_CHUNK, SSD_CHUNK
    g, j = SSD_GROUPS, SSD_HEADS // SSD_GROUPS
    xr = x.reshape(bsz, nc, q, g, j, SSD_HEAD_DIM)
    dtr = dt.reshape(bsz, nc, q, g, j)
    xdt = xr * dtr[..., None]
    a_cum = jnp.cumsum(dtr * a.reshape(g, j), axis=2)
    br = bm.reshape(bsz, nc, q, g, SSD_STATE)
    cr = cm.reshape(bsz, nc, q, g, SSD_STATE)
    lower_tri = jnp.asarray(np.tril(np.ones((q, q), dtype=bool)))[None, None, :, :, None, None]
    seg = a_cum[:, :, :, None] - a_cum[:, :, None, :]
    decay = jnp.exp(jnp.where(lower_tri, seg, -jnp.inf))
    cb = jnp.einsum('bclgn,bcsgn->bclsg', cr, br)
    y_diag = jnp.einsum('bclsgj,bcsgjp->bclgjp', cb[..., None] * decay, xdt)
    decay_to_end = jnp.exp(a_cum[:, :, -1:] - a_cum)
    chunk_states = jnp.einsum('bcsgn,bcsgjp->bcgjpn', br, xdt * decay_to_end[..., None])
    chunk_decay = jnp.exp(a_cum[:, :, -1])

    def carry_step(h, inp):
        s, d = inp
        return h * d[..., None, None] + s, h

    h0 = jnp.zeros_like(chunk_states[:, 0])
    _, h_prev = lax.scan(carry_step, h0, (jnp.moveaxis(chunk_states, 1, 0), jnp.moveaxis(chunk_decay, 1, 0)))
    h_prev = jnp.moveaxis(h_prev, 0, 1)
    y_off = jnp.einsum('bclgn,bcgjpn->bclgjp', cr, h_prev) * jnp.exp(a_cum)[..., None]
    return (y_diag + y_off).reshape(bsz, seqlen, SSD_HEADS, SSD_HEAD_DIM)


def ssd_mixer(z, xbc, dt_raw, conv_w, conv_b, dt_bias, a_log, d_skip, norm_g):
    bsz, seqlen = z.shape[0], z.shape[1]
    xbc = jax.nn.silu(centred_depthwise_conv(xbc, conv_w, conv_b))
    xs, bm, cm = jnp.split(xbc, [D_SSD, D_SSD + SSD_GROUPS * SSD_STATE], axis=-1)
    xs = xs.reshape(bsz, seqlen, SSD_HEADS, SSD_HEAD_DIM)
    bm = bm.reshape(bsz, seqlen, SSD_GROUPS, SSD_STATE)
    cm = cm.reshape(bsz, seqlen, SSD_GROUPS, SSD_STATE)
    dt_bias = dt_bias.astype(jnp.float32)
    a = -jnp.exp(a_log.astype(jnp.float32))
    dt_f = jax.nn.softplus(dt_raw[..., :SSD_HEADS] + dt_bias[0])
    dt_b = jax.nn.softplus(dt_raw[..., SSD_HEADS:] + dt_bias[1])
    flip = lambda t: jnp.flip(t, axis=1)
    y = ssd_chunked_scan(xs, dt_f, a[0], bm, cm)
    y = y + flip(ssd_chunked_scan(flip(xs), flip(dt_b), a[1], flip(bm), flip(cm)))
    y = y + xs * d_skip.astype(jnp.float32)[:, None]
    gated = (y.reshape(bsz, seqlen, D_SSD) * jax.nn.silu(z)).reshape(bsz, seqlen, SSD_GROUPS, D_SSD // SSD_GROUPS)
    gated = gated * lax.rsqrt(jnp.mean(gated * gated, axis=-1, keepdims=True) + EPS)
    return gated.reshape(bsz, seqlen, D_SSD) * norm_g.astype(jnp.float32)


def _linear_recurrence_op(e1, e2):
    a1, b1 = e1
    a2, b2 = e2
    return a1 * a2, a2 * b1 + b2


def s5_mixer(u, a_re, a_im, log_dt, b_re, b_im, c_re, c_im, d_skip, glu_w, glu_b):
    bsz, seqlen = u.shape[0], u.shape[1]
    ug = u.reshape(bsz, seqlen, S5_GROUPS, S5_GROUP)
    ugc = ug.astype(jnp.complex64)
    b_c = lax.complex(b_re.astype(jnp.float32), b_im.astype(jnp.float32))
    y = ug * d_skip.astype(jnp.float32)
    for direction in (0, 1):
        lam = lax.complex(a_re[direction].astype(jnp.float32), a_im[direction].astype(jnp.float32))
        dt = jnp.exp(log_dt[direction].astype(jnp.float32))[:, None]
        lam_bar = jnp.exp(lam * dt)
        b_bar = ((lam_bar - 1.0) / lam)[..., None] * b_c
        bu = jnp.einsum('gph,blgh->blgp', b_bar, ugc)
        lam_seq = jnp.broadcast_to(lam_bar, bu.shape)
        _, h = lax.associative_scan(_linear_recurrence_op, (lam_seq, bu), reverse=(direction == 1), axis=1)
        c_c = lax.complex(c_re[direction].astype(jnp.float32), c_im[direction].astype(jnp.float32))
        y = y + jnp.real(jnp.einsum('ghp,blgp->blgh', c_c, h))
    y = jax.nn.gelu(y.reshape(bsz, seqlen, D_S5))
    return y * jax.nn.sigmoid(y @ glu_w.astype(jnp.float32) + glu_b.astype(jnp.float32))


def neighbourhood_attention(q, k, v, rpb):
    bsz, seqlen = q.shape[0], q.shape[1]
    rows = seqlen // GRID_W
    kh = min(NA_KH_MAX, rows)
    ncb = GRID_W // NA_QB
    r = np.arange(rows)
    row_start = np.clip(r - kh // 2, 0, rows - kh)
    row_idx = row_start[:, None] + np.arange(kh)
    c0 = np.arange(ncb) * NA_QB
    key_col_start = np.clip(c0 - NA_KW // 2, 0, GRID_W - NA_KB)
    col_idx = key_col_start[:, None] + np.arange(NA_KB)
    q_col = c0[:, None] + np.arange(NA_QB)
    q_col_start = np.clip(q_col - NA_KW // 2, 0, GRID_W - NA_KW)
    valid = (col_idx[:, None, :] >= q_col_start[:, :, None]) & (col_idx[:, None, :] < q_col_start[:, :, None] + NA_KW)
    dc_idx = np.clip(col_idx[:, None, :] - q_col[:, :, None], -(NA_KW - 1), NA_KW - 1) + NA_KW - 1
    dr_idx = row_idx - r[:, None] + NA_KH_MAX - 1

    qg = q.reshape(bsz, rows, ncb, NA_QB, NA_HEADS, NA_HEAD_DIM)
    kgrid = k.reshape(bsz, rows, GRID_W, NA_HEADS, NA_HEAD_DIM)
    vgrid = v.reshape(bsz, rows, GRID_W, NA_HEADS, NA_HEAD_DIM)
    gr = row_idx[:, None, :, None]
    gc = col_idx[None, :, None, :]
    kg = kgrid[:, gr, gc]
    vg = vgrid[:, gr, gc]
    s = jnp.einsum('brcqhd,brckwhd->brchqkw', qg, kg) * (NA_HEAD_DIM ** -0.5)
    bias = rpb.astype(jnp.float32)[:, dr_idx[:, None, None, :, None], dc_idx[None, :, :, None, :]]
    s = s + jnp.moveaxis(bias, 0, 2)[None]
    s = jnp.where(jnp.asarray(valid)[None, None, :, None, :, None, :], s, -jnp.inf)
    p = jax.nn.softmax(s.reshape(s.shape[:-2] + (kh * NA_KB,)), axis=-1).reshape(s.shape)
    o = jnp.einsum('brchqkw,brckwhd->brcqhd', p, vg)
    return o.reshape(bsz, seqlen, D_NA)


def expert_choice_ffn(x, router_w, w_gate, w_up, w_down):
    bsz, seqlen, d = x.shape
    n_tok = bsz * seqlen
    cap = EC_CAPACITY * n_tok // N_EXPERTS
    xt = x.reshape(n_tok, d)
    aff = jax.nn.softmax((xt @ router_w).astype(jnp.float32), axis=-1)
    gate, idx = lax.top_k(aff.T, cap)
    xe = xt[idx]
    hid = jax.nn.silu(jnp.einsum('ecd,edf->ecf', xe, w_gate)) * jnp.einsum('ecd,edf->ecf', xe, w_up)
    ye = jnp.einsum('ecf,efd->ecd', hid, w_down) * gate[..., None].astype(x.dtype)
    out = jnp.zeros((n_tok, d), x.dtype).at[idx.reshape(-1)].add(ye.reshape(-1, d).astype(x.dtype))
    return out.reshape(bsz, seqlen, d)


def trunk(x, w):
    (norm_mix, w_in, conv_w, conv_b, ssd_dt_bias, ssd_a_log, ssd_d, ssd_norm,
     s5_a_re, s5_a_im, s5_log_dt, s5_b_re, s5_b_im, s5_c_re, s5_c_im, s5_d, s5_glu_w, s5_glu_b,
     na_rpb, w_out, norm_ffn, router_w, w_gate, w_up, w_down, final_norm) = w
    bsz, seqlen = x.shape[0], x.shape[1]
    splits = [D_SSD, D_SSD + D_XBC, D_SSD + D_XBC + 2 * SSD_HEADS, D_SSD + D_XBC + 2 * SSD_HEADS + D_S5]
    for l in range(DEPTH):
        h = rms_norm(x, norm_mix[l])
        proj = (h @ w_in[l]).astype(jnp.float32)
        z, xbc, dt_raw, u, qkv = jnp.split(proj, splits, axis=-1)
        y_ssd = ssd_mixer(z, xbc, dt_raw, conv_w[l], conv_b[l], ssd_dt_bias[l], ssd_a_log[l], ssd_d[l], ssd_norm[l])
        y_s5 = s5_mixer(u, s5_a_re[l], s5_a_im[l], s5_log_dt[l], s5_b_re[l], s5_b_im[l],
                        s5_c_re[l], s5_c_im[l], s5_d[l], s5_glu_w[l], s5_glu_b[l])
        qn, kn, vn = [t.reshape(bsz, seqlen, NA_HEADS, NA_HEAD_DIM) for t in jnp.split(qkv, 3, axis=-1)]
        y_na = neighbourhood_attention(qn, kn, vn, na_rpb[l])
        y = jnp.concatenate([y_ssd, y_s5, y_na], axis=-1).astype(x.dtype)
        x = x + y @ w_out[l]
        x = x + expert_choice_ffn(rms_norm(x, norm_ffn[l]), router_w[l], w_gate[l], w_up[l], w_down[l])
    return rms_norm(x, final_norm)


def setup_inputs(seed: int = 0) -> dict:
    key = jax.random.key(seed)
    ks = iter(jax.random.split(key, 40))
    f32 = jnp.float32

    def nrm(shape, scale):
        return scale * jax.random.normal(next(ks), shape, f32)

    def log_uniform(shape, lo, hi):
        return jax.random.uniform(next(ks), shape, f32, minval=math.log(lo), maxval=math.log(hi))

    x_prompt = jax.random.normal(next(ks), (BATCH, SEQ, D_MODEL), f32)
    x_sample = jax.random.normal(next(ks), (DEC_BATCH, DEC_SEQ, D_MODEL), f32)
    norm_mix = 1.0 + nrm((DEPTH, D_MODEL), 0.01)
    w_in = nrm((DEPTH, D_MODEL, D_IN_PROJ), D_MODEL ** -0.5)
    conv_w = nrm((DEPTH, SSD_CONV, D_XBC), SSD_CONV ** -0.5)
    conv_b = nrm((DEPTH, D_XBC), 0.01)
    dt0 = jnp.exp(log_uniform((DEPTH, 2, SSD_HEADS), 1e-3, 1e-1))
    ssd_dt_bias = dt0 + jnp.log(-jnp.expm1(-dt0))
    ssd_a_log = jnp.log(jax.random.uniform(next(ks), (DEPTH, 2, SSD_HEADS), f32, minval=1.0, maxval=16.0))
    ssd_d = 1.0 + nrm((DEPTH, SSD_HEADS), 0.01)
    ssd_norm = 1.0 + nrm((DEPTH, D_SSD), 0.01)
    n = jnp.arange(S5_STATE, dtype=f32)
    s5_a_re = -0.5 + nrm((DEPTH, 2, S5_GROUPS, S5_STATE), 0.01)
    s5_a_im = math.pi * n + nrm((DEPTH, 2, S5_GROUPS, S5_STATE), 0.01)
    s5_log_dt = log_uniform((DEPTH, 2, S5_GROUPS), 1e-3, 1e-1)
    s5_b_re = nrm((DEPTH, S5_GROUPS, S5_STATE, S5_GROUP), (2 * S5_GROUP) ** -0.5)
    s5_b_im = nrm((DEPTH, S5_GROUPS, S5_STATE, S5_GROUP), (2 * S5_GROUP) ** -0.5)
    s5_c_re = nrm((DEPTH, 2, S5_GROUPS, S5_GROUP, S5_STATE), (2 * S5_STATE) ** -0.5)
    s5_c_im = nrm((DEPTH, 2, S5_GROUPS, S5_GROUP, S5_STATE), (2 * S5_STATE) ** -0.5)
    s5_d = nrm((DEPTH, S5_GROUPS, S5_GROUP), 1.0)
    s5_glu_w = nrm((DEPTH, D_S5, D_S5), D_S5 ** -0.5)
    s5_glu_b = nrm((DEPTH, D_S5), 0.01)
    na_rpb = nrm((DEPTH, NA_HEADS, 2 * NA_KH_MAX - 1, 2 * NA_KW - 1), 0.02)
    w_out = nrm((DEPTH, D_MIX, D_MODEL), D_MIX ** -0.5)
    norm_ffn = 1.0 + nrm((DEPTH, D_MODEL), 0.01)
    router_w = nrm((DEPTH, D_MODEL, N_EXPERTS), D_MODEL ** -0.5)
    w_gate = nrm((DEPTH, N_EXPERTS, D_MODEL, EXPERT_FF), D_MODEL ** -0.5)
    w_up = nrm((DEPTH, N_EXPERTS, D_MODEL, EXPERT_FF), D_MODEL ** -0.5)
    w_down = nrm((DEPTH, N_EXPERTS, EXPERT_FF, D_MODEL), EXPERT_FF ** -0.5)
    final_norm = 1.0 + nrm((D_MODEL,), 0.01)
    return {"x_prompt": x_prompt, "x_sample": x_sample, "norm_mix": norm_mix, "w_in": w_in,
            "conv_w": conv_w, "conv_b": conv_b, "ssd_dt_bias": ssd_dt_bias, "ssd_a_log": ssd_a_log,
            "ssd_d": ssd_d, "ssd_norm": ssd_norm, "s5_a_re": s5_a_re, "s5_a_im": s5_a_im,
            "s5_log_dt": s5_log_dt, "s5_b_re": s5_b_re, "s5_b_im": s5_b_im, "s5_c_re": s5_c_re,
            "s5_c_im": s5_c_im, "s5_d": s5_d, "s5_glu_w": s5_glu_w, "s5_glu_b": s5_glu_b,
            "na_rpb": na_rpb, "w_out": w_out, "norm_ffn": norm_ffn, "router_w": router_w,
            "w_gate": w_gate, "w_up": w_up, "w_down": w_down, "final_norm": final_norm}


def reference(x_prompt, x_sample, norm_mix, w_in, conv_w, conv_b, ssd_dt_bias, ssd_a_log, ssd_d, ssd_norm,
              s5_a_re, s5_a_im, s5_log_dt, s5_b_re, s5_b_im, s5_c_re, s5_c_im, s5_d, s5_glu_w, s5_glu_b,
              na_rpb, w_out, norm_ffn, router_w, w_gate, w_up, w_down, final_norm):
    weights = (norm_mix, w_in, conv_w, conv_b, ssd_dt_bias, ssd_a_log, ssd_d, ssd_norm,
               s5_a_re, s5_a_im, s5_log_dt, s5_b_re, s5_b_im, s5_c_re, s5_c_im, s5_d, s5_glu_w, s5_glu_b,
               na_rpb, w_out, norm_ffn, router_w, w_gate, w_up, w_down, final_norm)
    y_prompt = trunk(x_prompt, weights)
    y_sample = trunk(x_sample, weights)
    return (y_prompt, y_sample)
```

```python
import functools
import math

import jax
import jax.numpy as jnp
import numpy as np
from jax import lax
from jax.experimental import pallas as pl
from jax.experimental.pallas import tpu as pltpu

D_MODEL = 2048
DEPTH = 2
EPS = 1e-6
SSD_HEADS = 16
SSD_HEAD_DIM = 64
D_SSD = SSD_HEADS * SSD_HEAD_DIM
SSD_GROUPS = 2
SSD_STATE = 128
SSD_CONV = 5
SSD_CHUNK = 128
D_XBC = D_SSD + 2 * SSD_GROUPS * SSD_STATE
S5_GROUP = 16
S5_GROUPS = 32
D_S5 = S5_GROUP * S5_GROUPS
S5_STATE = 64
NA_HEADS = 8
NA_HEAD_DIM = 64
D_NA = NA_HEADS * NA_HEAD_DIM
GRID_W = 64
NA_KH_MAX = 8
NA_KW = 16
NA_QB = 16
NA_KB = 32
N_EXPERTS = 16
EXPERT_FF = 2048
EC_CAPACITY = 2

VMEM_LIMIT_BYTES = 56 * 1024 * 1024


def _mm_kernel(x_ref, w_ref, o_ref):
    o_ref[...] = jnp.dot(x_ref[...].astype(jnp.bfloat16), w_ref[...],
                         preferred_element_type=jnp.float32)


def _matmul(x, w, tm=512, tn=512):
    m, k = x.shape
    n = w.shape[1]
    tn = min(tn, n)
    return pl.pallas_call(
        _mm_kernel,
        grid=(m // tm, n // tn),
        in_specs=[pl.BlockSpec((tm, k), lambda i, j: (i, 0)),
                  pl.BlockSpec((k, tn), lambda i, j: (0, j))],
        out_specs=pl.BlockSpec((tm, tn), lambda i, j: (i, j)),
        out_shape=jax.ShapeDtypeStruct((m, n), jnp.float32),
        compiler_params=pltpu.CompilerParams(
            dimension_semantics=("parallel", "arbitrary"),
            vmem_limit_bytes=VMEM_LIMIT_BYTES),
        name="matmul",
    )(x, w)


def _norm_mm_kernel(x_ref, g_ref, w_ref, o_ref, hn_ref):
    @pl.when(pl.program_id(1) == 0)
    def _():
        xf = x_ref[...]
        y = xf * lax.rsqrt(jnp.mean(xf * xf, axis=-1, keepdims=True) + EPS)
        hn_ref[...] = (y * g_ref[...]).astype(jnp.bfloat16)

    o_ref[...] = jnp.dot(hn_ref[...], w_ref[...], preferred_element_type=jnp.float32)


def _norm_matmul(x, g, w, tm=512, tn=512):
    m, k = x.shape
    n = w.shape[1]
    tn = min(tn, n)
    return pl.pallas_call(
        _norm_mm_kernel,
        grid=(m // tm, n // tn),
        in_specs=[pl.BlockSpec((tm, k), lambda i, j: (i, 0)),
                  pl.BlockSpec((1, k), lambda i, j: (0, 0)),
                  pl.BlockSpec((k, tn), lambda i, j: (0, j))],
        out_specs=pl.BlockSpec((tm, tn), lambda i, j: (i, j)),
        out_shape=jax.ShapeDtypeStruct((m, n), jnp.float32),
        scratch_shapes=[pltpu.VMEM((tm, k), jnp.bfloat16)],
        compiler_params=pltpu.CompilerParams(
            dimension_semantics=("parallel", "arbitrary"),
            vmem_limit_bytes=VMEM_LIMIT_BYTES),
        name="norm_matmul",
    )(x, g.reshape(1, k).astype(jnp.float32), w)


def _mm_res_kernel(x_ref, w_ref, r_ref, o_ref):
    o_ref[...] = r_ref[...] + jnp.dot(x_ref[...].astype(jnp.bfloat16), w_ref[...],
                                      preferred_element_type=jnp.float32)


def _matmul_residual(x, w, r, tm=512, tn=512):
    m, k = x.shape
    n = w.shape[1]
    return pl.pallas_call(
        _mm_res_kernel,
        grid=(m // tm, n // tn),
        in_specs=[pl.BlockSpec((tm, k), lambda i, j: (i, 0)),
                  pl.BlockSpec((k, tn), lambda i, j: (0, j)),
                  pl.BlockSpec((tm, tn), lambda i, j: (i, j))],
        out_specs=pl.BlockSpec((tm, tn), lambda i, j: (i, j)),
        out_shape=jax.ShapeDtypeStruct((m, n), jnp.float32),
        compiler_params=pltpu.CompilerParams(
            dimension_semantics=("parallel", "arbitrary"),
            vmem_limit_bytes=VMEM_LIMIT_BYTES),
        name="matmul_residual",
    )(x, w, r)


def _ffn_kernel(x_ref, wg_ref, wu_ref, wd_ref, gate_ref, o_ref, acc_ref):
    f = pl.program_id(2)

    @pl.when(f == 0)
    def _():
        acc_ref[...] = jnp.zeros_like(acc_ref)

    x = x_ref[0]
    hg = jnp.dot(x, wg_ref[0], preferred_element_type=jnp.float32)
    hu = jnp.dot(x, wu_ref[0], preferred_element_type=jnp.float32)
    hid = (hg * jax.nn.sigmoid(hg) * hu).astype(jnp.bfloat16)
    acc_ref[...] += jnp.dot(hid, wd_ref[0], preferred_element_type=jnp.float32)

    @pl.when(f == pl.num_programs(2) - 1)
    def _():
        o_ref[0] = acc_ref[...] * gate_ref[0]


def _expert_ffn(xe, wg, wu, wd, gate, tm=1024, tf=256):
    e, c, d = xe.shape
    f = wg.shape[2]
    return pl.pallas_call(
        _ffn_kernel,
        grid=(e, c // tm, f // tf),
        in_specs=[pl.BlockSpec((1, tm, d), lambda ei, i, j: (ei, i, 0)),
                  pl.BlockSpec((1, d, tf), lambda ei, i, j: (ei, 0, j)),
                  pl.BlockSpec((1, d, tf), lambda ei, i, j: (ei, 0, j)),
                  pl.BlockSpec((1, tf, d), lambda ei, i, j: (ei, j, 0)),
                  pl.BlockSpec((1, tm, 1), lambda ei, i, j: (ei, i, 0))],
        out_specs=pl.BlockSpec((1, tm, d), lambda ei, i, j: (ei, i, 0)),
        out_shape=jax.ShapeDtypeStruct((e, c, d), jnp.float32),
        scratch_shapes=[pltpu.VMEM((tm, d), jnp.float32)],
        compiler_params=pltpu.CompilerParams(
            dimension_semantics=("parallel", "parallel", "arbitrary"),
            vmem_limit_bytes=VMEM_LIMIT_BYTES),
        name="expert_ffn",
    )(xe, wg, wu, wd, gate)


def _rms_norm(x, g):
    xf = x.astype(jnp.float32)
    y = xf * lax.rsqrt(jnp.mean(xf * xf, axis=-1, keepdims=True) + EPS)
    return (y * g.astype(jnp.float32)).astype(x.dtype)


def _centred_depthwise_conv(x, w, b):
    k = w.shape[0]
    left = k // 2
    y = lax.conv_general_dilated(
        x, w.astype(x.dtype)[:, None, :], window_strides=(1,), padding=[(left, k - 1 - left)],
        dimension_numbers=('NWC', 'WIO', 'NWC'), feature_group_count=x.shape[-1])
    return y + b.astype(x.dtype)


def _ssd_chunked_scan(x, dt, a, bm, cm):
    bsz, seqlen = x.shape[0], x.shape[1]
    nc, q = seqlen // SSD_CHUNK, SSD_CHUNK
    g, j = SSD_GROUPS, SSD_HEADS // SSD_GROUPS
    xr = x.reshape(bsz, nc, q, g, j, SSD_HEAD_DIM)
    dtr = dt.reshape(bsz, nc, q, g, j)
    xdt = xr * dtr[..., None]
    a_cum = jnp.cumsum(dtr * a.reshape(g, j), axis=2)
    br = bm.reshape(bsz, nc, q, g, SSD_STATE)
    cr = cm.reshape(bsz, nc, q, g, SSD_STATE)
    lower_tri = jnp.asarray(np.tril(np.ones((q, q), dtype=bool)))[None, None, :, :, None, None]
    seg = a_cum[:, :, :, None] - a_cum[:, :, None, :]
    decay = jnp.exp(jnp.where(lower_tri, seg, -jnp.inf))
    cb = jnp.einsum('bclgn,bcsgn->bclsg', cr, br)
    y_diag = jnp.einsum('bclsgj,bcsgjp->bclgjp', cb[..., None] * decay, xdt)
    decay_to_end = jnp.exp(a_cum[:, :, -1:] - a_cum)
    chunk_states = jnp.einsum('bcsgn,bcsgjp->bcgjpn', br, xdt * decay_to_end[..., None])
    chunk_decay = jnp.exp(a_cum[:, :, -1])

    def carry_step(h, inp):
        s, d = inp
        return h * d[..., None, None] + s, h

    h0 = jnp.zeros_like(chunk_states[:, 0])
    _, h_prev = lax.scan(carry_step, h0, (jnp.moveaxis(chunk_states, 1, 0), jnp.moveaxis(chunk_decay, 1, 0)))
    h_prev = jnp.moveaxis(h_prev, 0, 1)
    y_off = jnp.einsum('bclgn,bcgjpn->bclgjp', cr, h_prev) * jnp.exp(a_cum)[..., None]
    return (y_diag + y_off).reshape(bsz, seqlen, SSD_HEADS, SSD_HEAD_DIM)


def _ssd_mixer(z, xbc, dt_raw, conv_w, conv_b, dt_bias, a_log, d_skip, norm_g):
    bsz, seqlen = z.shape[0], z.shape[1]
    xbc = jax.nn.silu(_centred_depthwise_conv(xbc, conv_w, conv_b))
    xs, bm, cm = jnp.split(xbc, [D_SSD, D_SSD + SSD_GROUPS * SSD_STATE], axis=-1)
    xs = xs.reshape(bsz, seqlen, SSD_HEADS, SSD_HEAD_DIM)
    bm = bm.reshape(bsz, seqlen, SSD_GROUPS, SSD_STATE)
    cm = cm.reshape(bsz, seqlen, SSD_GROUPS, SSD_STATE)
    dt_bias = dt_bias.astype(jnp.float32)
    a = -jnp.exp(a_log.astype(jnp.float32))
    dt_f = jax.nn.softplus(dt_raw[..., :SSD_HEADS] + dt_bias[0])
    dt_b = jax.nn.softplus(dt_raw[..., SSD_HEADS:] + dt_bias[1])
    flip = lambda t: jnp.flip(t, axis=1)
    y = _ssd_chunked_scan(xs, dt_f, a[0], bm, cm)
    y = y + flip(_ssd_chunked_scan(flip(xs), flip(dt_b), a[1], flip(bm), flip(cm)))
    y = y + xs * d_skip.astype(jnp.float32)[:, None]
    gated = (y.reshape(bsz, seqlen, D_SSD) * jax.nn.silu(z)).reshape(bsz, seqlen, SSD_GROUPS, D_SSD // SSD_GROUPS)
    gated = gated * lax.rsqrt(jnp.mean(gated * gated, axis=-1, keepdims=True) + EPS)
    return gated.reshape(bsz, seqlen, D_SSD) * norm_g.astype(jnp.float32)


def _linear_recurrence_op(e1, e2):
    a1, b1 = e1
    a2, b2 = e2
    return a1 * a2, a2 * b1 + b2


def _s5_mixer(u, a_re, a_im, log_dt, b_re, b_im, c_re, c_im, d_skip, glu_w, glu_b):
    bsz, seqlen = u.shape[0], u.shape[1]
    ug = u.reshape(bsz, seqlen, S5_GROUPS, S5_GROUP)
    ugc = ug.astype(jnp.complex64)
    b_c = lax.complex(b_re.astype(jnp.float32), b_im.astype(jnp.float32))
    y = ug * d_skip.astype(jnp.float32)
    for direction in (0, 1):
        lam = lax.complex(a_re[direction].astype(jnp.float32), a_im[direction].astype(jnp.float32))
        dt = jnp.exp(log_dt[direction].astype(jnp.float32))[:, None]
        lam_bar = jnp.exp(lam * dt)
        b_bar = ((lam_bar - 1.0) / lam)[..., None] * b_c
        bu = jnp.einsum('gph,blgh->blgp', b_bar, ugc)
        lam_seq = jnp.broadcast_to(lam_bar, bu.shape)
        _, h = lax.associative_scan(_linear_recurrence_op, (lam_seq, bu), reverse=(direction == 1), axis=1)
        c_c = lax.complex(c_re[direction].astype(jnp.float32), c_im[direction].astype(jnp.float32))
        y = y + jnp.real(jnp.einsum('ghp,blgp->blgh', c_c, h))
    y = jax.nn.gelu(y.reshape(bsz, seqlen, D_S5))
    return y * jax.nn.sigmoid(y @ glu_w.astype(jnp.float32) + glu_b.astype(jnp.float32))


def _neighbourhood_attention(q, k, v, rpb):
    bsz, seqlen = q.shape[0], q.shape[1]
    rows = seqlen // GRID_W
    kh = min(NA_KH_MAX, rows)
    ncb = GRID_W // NA_QB
    r = np.arange(rows)
    row_start = np.clip(r - kh // 2, 0, rows - kh)
    row_idx = row_start[:, None] + np.arange(kh)
    c0 = np.arange(ncb) * NA_QB
    key_col_start = np.clip(c0 - NA_KW // 2, 0, GRID_W - NA_KB)
    col_idx = key_col_start[:, None] + np.arange(NA_KB)
    q_col = c0[:, None] + np.arange(NA_QB)
    q_col_start = np.clip(q_col - NA_KW // 2, 0, GRID_W - NA_KW)
    valid = (col_idx[:, None, :] >= q_col_start[:, :, None]) & (col_idx[:, None, :] < q_col_start[:, :, None] + NA_KW)
    dc_idx = np.clip(col_idx[:, None, :] - q_col[:, :, None], -(NA_KW - 1), NA_KW - 1) + NA_KW - 1
    dr_idx = row_idx - r[:, None] + NA_KH_MAX - 1

    qg = q.reshape(bsz, rows, ncb, NA_QB, NA_HEADS, NA_HEAD_DIM)
    kgrid = k.reshape(bsz, rows, GRID_W, NA_HEADS, NA_HEAD_DIM)
    vgrid = v.reshape(bsz, rows, GRID_W, NA_HEADS, NA_HEAD_DIM)
    gr = row_idx[:, None, :, None]
    gc = col_idx[None, :, None, :]
    kg = kgrid[:, gr, gc]
    vg = vgrid[:, gr, gc]
    s = jnp.einsum('brcqhd,brckwhd->brchqkw', qg, kg) * (NA_HEAD_DIM ** -0.5)
    bias = rpb.astype(jnp.float32)[:, dr_idx[:, None, None, :, None], dc_idx[None, :, :, None, :]]
    s = s + jnp.moveaxis(bias, 0, 2)[None]
    s = jnp.where(jnp.asarray(valid)[None, None, :, None, :, None, :], s, -jnp.inf)
    p = jax.nn.softmax(s.reshape(s.shape[:-2] + (kh * NA_KB,)), axis=-1).reshape(s.shape)
    o = jnp.einsum('brchqkw,brckwhd->brcqhd', p, vg)
    return o.reshape(bsz, seqlen, D_NA)


def _expert_choice_ffn(xn, router_w, wg, wu, wd):
    n_tok, d = xn.shape
    cap = EC_CAPACITY * n_tok // N_EXPERTS
    aff = jax.nn.softmax(jnp.dot(xn, router_w, precision=lax.Precision.HIGHEST), axis=-1)
    gate, idx = lax.top_k(aff.T, cap)
    xe = xn.astype(jnp.bfloat16)[idx]
    ye = _expert_ffn(xe, wg, wu, wd, gate[..., None])
    out = jnp.zeros((n_tok, d), jnp.float32).at[idx.reshape(-1)].add(ye.reshape(-1, d))
    return out


def _trunk(x, w):
    (norm_mix, w_in, conv_w, conv_b, ssd_dt_bias, ssd_a_log, ssd_d, ssd_norm,
     s5_a_re, s5_a_im, s5_log_dt, s5_b_re, s5_b_im, s5_c_re, s5_c_im, s5_d, s5_glu_w, s5_glu_b,
     na_rpb, w_out, norm_ffn, router_w, w_gate, w_up, w_down, final_norm) = w
    bsz, seqlen = x.shape[0], x.shape[1]
    n_tok = bsz * seqlen
    x = x.reshape(n_tok, D_MODEL)
    for l in range(DEPTH):
        pm = _norm_matmul(x, norm_mix[l], w_in[l][0])
        dt_raw = _norm_matmul(x, norm_mix[l], w_in[l][1])[:, :2 * SSD_HEADS]
        pm = pm.reshape(bsz, seqlen, -1)
        dt_raw = dt_raw.reshape(bsz, seqlen, -1)
        z = pm[..., :D_SSD]
        xbc = pm[..., D_SSD:D_SSD + D_XBC]
        u = pm[..., D_SSD + D_XBC:D_SSD + D_XBC + D_S5]
        qkv = pm[..., D_SSD + D_XBC + D_S5:]
        y_ssd = _ssd_mixer(z, xbc, dt_raw, conv_w[l], conv_b[l], ssd_dt_bias[l], ssd_a_log[l], ssd_d[l], ssd_norm[l])
        y_s5 = _s5_mixer(u, s5_a_re[l], s5_a_im[l], s5_log_dt[l], s5_b_re[l], s5_b_im[l],
                         s5_c_re[l], s5_c_im[l], s5_d[l], s5_glu_w[l], s5_glu_b[l])
        qn, kn, vn = [t.reshape(bsz, seqlen, NA_HEADS, NA_HEAD_DIM) for t in jnp.split(qkv, 3, axis=-1)]
        y_na = _neighbourhood_attention(qn, kn, vn, na_rpb[l])
        y = jnp.concatenate([y_ssd, y_s5, y_na], axis=-1).reshape(n_tok, D_MODEL)
        x = _matmul_residual(y, w_out[l], x)
        xn = _rms_norm(x, norm_ffn[l])
        x = x + _expert_choice_ffn(xn, router_w[l], w_gate[l], w_up[l], w_down[l])
    return _rms_norm(x, final_norm).reshape(bsz, seqlen, D_MODEL)


def _split_w_in(w_in):
    o0 = D_SSD + D_XBC
    main = jnp.concatenate([w_in[..., :o0], w_in[..., o0 + 2 * SSD_HEADS:]], axis=-1).astype(jnp.bfloat16)
    dtw = jnp.pad(w_in[..., o0:o0 + 2 * SSD_HEADS], ((0, 0), (0, 0), (0, 128 - 2 * SSD_HEADS))).astype(jnp.bfloat16)
    return [(main[l], dtw[l]) for l in range(DEPTH)]


def kernel(x_prompt, x_sample, norm_mix, w_in, conv_w, conv_b, ssd_dt_bias, ssd_a_log, ssd_d, ssd_norm,
           s5_a_re, s5_a_im, s5_log_dt, s5_b_re, s5_b_im, s5_c_re, s5_c_im, s5_d, s5_glu_w, s5_glu_b,
           na_rpb, w_out, norm_ffn, router_w, w_gate, w_up, w_down, final_norm):
    bf = jnp.bfloat16
    weights = (norm_mix, _split_w_in(w_in), conv_w, conv_b, ssd_dt_bias, ssd_a_log, ssd_d, ssd_norm,
               s5_a_re, s5_a_im, s5_log_dt, s5_b_re, s5_b_im, s5_c_re, s5_c_im, s5_d, s5_glu_w, s5_glu_b,
               na_rpb, w_out.astype(bf), norm_ffn, router_w, w_gate.astype(bf), w_up.astype(bf),
               w_down.astype(bf), final_norm)
    return (_trunk(x_prompt, weights), _trunk(x_sample, weights))
```

```python
import functools
import math

import jax
import jax.numpy as jnp
import numpy as np
from jax import lax
from jax.experimental import pallas as pl
from jax.experimental.pallas import tpu as pltpu

D_MODEL = 2048
DEPTH = 2
EPS = 1e-6
SSD_HEADS = 16
SSD_HEAD_DIM = 64
D_SSD = SSD_HEADS * SSD_HEAD_DIM
SSD_GROUPS = 2
SSD_STATE = 128
SSD_CONV = 5
SSD_CHUNK = 128
D_XBC = D_SSD + 2 * SSD_GROUPS * SSD_STATE
S5_GROUP = 16
S5_GROUPS = 32
D_S5 = S5_GROUP * S5_GROUPS
S5_STATE = 64
NA_HEADS = 8
NA_HEAD_DIM = 64
D_NA = NA_HEADS * NA_HEAD_DIM
GRID_W = 64
NA_KH_MAX = 8
NA_KW = 16
NA_QB = 16
NA_KB = 32
N_EXPERTS = 16
EXPERT_FF = 2048
EC_CAPACITY = 2

VMEM_LIMIT_BYTES = 56 * 1024 * 1024


def _mm_kernel(x_ref, w_ref, o_ref):
    o_ref[...] = jnp.dot(x_ref[...].astype(jnp.bfloat16), w_ref[...],
                         preferred_element_type=jnp.float32)


def _matmul(x, w, tm=512, tn=512):
    m, k = x.shape
    n = w.shape[1]
    tn = min(tn, n)
    return pl.pallas_call(
        _mm_kernel,
        grid=(m // tm, n // tn),
        in_specs=[pl.BlockSpec((tm, k), lambda i, j: (i, 0)),
                  pl.BlockSpec((k, tn), lambda i, j: (0, j))],
        out_specs=pl.BlockSpec((tm, tn), lambda i, j: (i, j)),
        out_shape=jax.ShapeDtypeStruct((m, n), jnp.float32),
        compiler_params=pltpu.CompilerParams(
            dimension_semantics=("parallel", "arbitrary"),
            vmem_limit_bytes=VMEM_LIMIT_BYTES),
        name="matmul",
    )(x, w)


def _norm_mm_kernel(x_ref, g_ref, w_ref, o_ref, hn_ref):
    @pl.when(pl.program_id(1) == 0)
    def _():
        xf = x_ref[...]
        y = xf * lax.rsqrt(jnp.mean(xf * xf, axis=-1, keepdims=True) + EPS)
        hn_ref[...] = (y * g_ref[...]).astype(jnp.bfloat16)

    o_ref[...] = jnp.dot(hn_ref[...], w_ref[...], preferred_element_type=jnp.float32)


def _norm_matmul(x, g, w, tm=512, tn=512):
    m, k = x.shape
    n = w.shape[1]
    tn = min(tn, n)
    return pl.pallas_call(
        _norm_mm_kernel,
        grid=(m // tm, n // tn),
        in_specs=[pl.BlockSpec((tm, k), lambda i, j: (i, 0)),
                  pl.BlockSpec((1, k), lambda i, j: (0, 0)),
                  pl.BlockSpec((k, tn), lambda i, j: (0, j))],
        out_specs=pl.BlockSpec((tm, tn), lambda i, j: (i, j)),
        out_shape=jax.ShapeDtypeStruct((m, n), jnp.float32),
        scratch_shapes=[pltpu.VMEM((tm, k), jnp.bfloat16)],
        compiler_params=pltpu.CompilerParams(
            dimension_semantics=("parallel", "arbitrary"),
            vmem_limit_bytes=VMEM_LIMIT_BYTES),
        name="norm_matmul",
    )(x, g.reshape(1, k).astype(jnp.float32), w)


def _out_proj_kernel(ya_ref, yb_ref, yc_ref, wa_ref, wb_ref, wc_ref, r_ref, o_ref):
    bf = jnp.bfloat16
    acc = jnp.dot(ya_ref[...].astype(bf), wa_ref[...], preferred_element_type=jnp.float32)
    acc += jnp.dot(yb_ref[...].astype(bf), wb_ref[...], preferred_element_type=jnp.float32)
    acc += jnp.dot(yc_ref[...].astype(bf), wc_ref[...], preferred_element_type=jnp.float32)
    o_ref[...] = r_ref[...] + acc


def _out_proj(y_ssd, y_s5, y_na, w, r, tm=512, tn=512):
    m = r.shape[0]
    n = w.shape[1]
    ka, kb, kc = y_ssd.shape[1], y_s5.shape[1], y_na.shape[1]
    wa, wb, wc = w[:ka], w[ka:ka + kb], w[ka + kb:]
    xspec = lambda k: pl.BlockSpec((tm, k), lambda i, j: (i, 0))
    wspec = lambda k: pl.BlockSpec((k, tn), lambda i, j: (0, j))
    return pl.pallas_call(
        _out_proj_kernel,
        grid=(m // tm, n // tn),
        in_specs=[xspec(ka), xspec(kb), xspec(kc), wspec(ka), wspec(kb), wspec(kc),
                  pl.BlockSpec((tm, tn), lambda i, j: (i, j))],
        out_specs=pl.BlockSpec((tm, tn), lambda i, j: (i, j)),
        out_shape=jax.ShapeDtypeStruct((m, n), jnp.float32),
        compiler_params=pltpu.CompilerParams(
            dimension_semantics=("parallel", "arbitrary"),
            vmem_limit_bytes=VMEM_LIMIT_BYTES),
        name="out_proj",
    )(y_ssd, y_s5, y_na, wa, wb, wc, r)


def _ffn_kernel(x_ref, wg_ref, wu_ref, wd_ref, gate_ref, o_ref, acc_ref):
    f = pl.program_id(2)

    @pl.when(f == 0)
    def _():
        acc_ref[...] = jnp.zeros_like(acc_ref)

    x = x_ref[0]
    hg = jnp.dot(x, wg_ref[0], preferred_element_type=jnp.float32)
    hu = jnp.dot(x, wu_ref[0], preferred_element_type=jnp.float32)
    hid = (hg * jax.nn.sigmoid(hg) * hu).astype(jnp.bfloat16)
    acc_ref[...] += jnp.dot(hid, wd_ref[0], preferred_element_type=jnp.float32)

    @pl.when(f == pl.num_programs(2) - 1)
    def _():
        o_ref[0] = acc_ref[...] * gate_ref[0]


def _expert_ffn(xe, wg, wu, wd, gate, tm=1024, tf=256):
    e, c, d = xe.shape
    f = wg.shape[2]
    return pl.pallas_call(
        _ffn_kernel,
        grid=(e, c // tm, f // tf),
        in_specs=[pl.BlockSpec((1, tm, d), lambda ei, i, j: (ei, i, 0)),
                  pl.BlockSpec((1, d, tf), lambda ei, i, j: (ei, 0, j)),
                  pl.BlockSpec((1, d, tf), lambda ei, i, j: (ei, 0, j)),
                  pl.BlockSpec((1, tf, d), lambda ei, i, j: (ei, j, 0)),
                  pl.BlockSpec((1, tm, 1), lambda ei, i, j: (ei, i, 0))],
        out_specs=pl.BlockSpec((1, tm, d), lambda ei, i, j: (ei, i, 0)),
        out_shape=jax.ShapeDtypeStruct((e, c, d), jnp.float32),
        scratch_shapes=[pltpu.VMEM((tm, d), jnp.float32)],
        compiler_params=pltpu.CompilerParams(
            dimension_semantics=("parallel", "parallel", "arbitrary"),
            vmem_limit_bytes=VMEM_LIMIT_BYTES),
        name="expert_ffn",
    )(xe, wg, wu, wd, gate)


S5_Q = 64
S5_X = S5_Q * S5_GROUP
S5_NS = 4 * S5_STATE


def _s5_operators(a_re, a_im, log_dt, b_re, b_im, c_re, c_im, d_skip):
    f32 = jnp.float32
    hp = lax.Precision.HIGHEST
    q = S5_Q
    lam = lax.complex(a_re.astype(f32), a_im.astype(f32))
    dt = jnp.exp(log_dt.astype(f32))[..., None]
    ldt = lam * dt
    lam_bar = jnp.exp(ldt)
    b_c = lax.complex(b_re.astype(f32), b_im.astype(f32))
    b_bar = ((lam_bar - 1.0) / lam)[..., None] * b_c[None]
    c_c = lax.complex(c_re.astype(f32), c_im.astype(f32))
    steps = jnp.arange(q + 1, dtype=f32)
    pw = jnp.exp(ldt[:, :, None, :] * steps[None, None, :, None])
    kern = jnp.real(jnp.einsum('dgop,dgkp,dgpi->dgkoi', c_c, pw[:, :, :q], b_bar, precision=hp))
    s_idx = np.arange(q)[:, None]
    t_idx = np.arange(q)[None, :]
    diff = t_idx - s_idx
    t_f = kern[0][:, np.maximum(diff, 0)] * jnp.asarray(diff >= 0, f32)[None, :, :, None, None]
    t_b = kern[1][:, np.maximum(-diff, 0)] * jnp.asarray(diff <= 0, f32)[None, :, :, None, None]
    skip = (jnp.asarray(np.eye(q), f32)[None, :, :, None, None]
            * (jnp.eye(S5_GROUP, dtype=f32)[None] * d_skip.astype(f32)[:, None, :])[:, None, None])
    tmat = (t_f + t_b + skip).transpose(0, 1, 4, 2, 3).reshape(S5_GROUPS, S5_X, S5_X)
    p_f = pw[0][:, ::-1][:, 1:, None, :] * jnp.swapaxes(b_bar[0], 1, 2)[:, None]
    p_b = pw[1][:, :q, None, :] * jnp.swapaxes(b_bar[1], 1, 2)[:, None]
    pmat = jnp.concatenate([jnp.real(p_f), jnp.real(p_b), jnp.imag(p_f), jnp.imag(p_b)], axis=-1)
    pmat = pmat.reshape(S5_GROUPS, S5_X, S5_NS)
    o_f = jnp.swapaxes(c_c[0], 1, 2)[:, :, None, :] * jnp.swapaxes(pw[0][:, 1:], 1, 2)[..., None]
    o_b = jnp.swapaxes(c_c[1], 1, 2)[:, :, None, :] * jnp.swapaxes(pw[1][:, ::-1][:, :q], 1, 2)[..., None]
    omat = jnp.concatenate([jnp.real(o_f), jnp.real(o_b), -jnp.imag(o_f), -jnp.imag(o_b)], axis=1)
    omat = omat.reshape(S5_GROUPS, S5_NS, S5_X)
    lam_q = pw[:, :, q]
    lq_re = jnp.concatenate([jnp.real(lam_q[0]), jnp.real(lam_q[1])], axis=-1)[:, None, :]
    lq_im = jnp.concatenate([jnp.imag(lam_q[0]), jnp.imag(lam_q[1])], axis=-1)[:, None, :]
    wmat = jnp.concatenate([tmat, pmat], axis=-1).astype(jnp.bfloat16)
    return wmat, omat.astype(jnp.bfloat16), lq_re, lq_im


def _s5_scan_kernel(u_ref, w_ref, o_ref, lre_ref, lim_ref, y_ref, sre_ref, sim_ref, hre_ref, him_ref, *, nc, nb):
    half = 2 * S5_STATE
    u = u_ref[0]
    s = jnp.dot(u, w_ref[0, :, S5_X:], preferred_element_type=jnp.float32)
    sre_ref[...] = s[:, :half]
    sim_ref[...] = s[:, half:]
    lre = jnp.broadcast_to(lre_ref[0], (nb, half))
    lim = jnp.broadcast_to(lim_ref[0], (nb, half))
    fwd_lane = lax.broadcasted_iota(jnp.int32, (nb, half), 1) < S5_STATE

    def step(c, carry):
        hr, hi = carry
        rf = pl.multiple_of(c * nb, nb)
        rb = pl.multiple_of((nc - 1 - c) * nb, nb)
        hre_ref[pl.ds(rf, nb), :S5_STATE] = hr[:, :S5_STATE]
        him_ref[pl.ds(rf, nb), :S5_STATE] = hi[:, :S5_STATE]
        hre_ref[pl.ds(rb, nb), S5_STATE:] = hr[:, S5_STATE:]
        him_ref[pl.ds(rb, nb), S5_STATE:] = hi[:, S5_STATE:]
        sr = jnp.where(fwd_lane, sre_ref[pl.ds(rf, nb), :], sre_ref[pl.ds(rb, nb), :])
        si = jnp.where(fwd_lane, sim_ref[pl.ds(rf, nb), :], sim_ref[pl.ds(rb, nb), :])
        return lre * hr - lim * hi + sr, lre * hi + lim * hr + si

    zero = jnp.zeros((nb, half), jnp.float32)
    lax.fori_loop(0, nc, step, (zero, zero))
    h = jnp.concatenate([hre_ref[...], him_ref[...]], axis=1).astype(jnp.bfloat16)
    y = jnp.dot(u, w_ref[0, :, :S5_X], preferred_element_type=jnp.float32)
    y_ref[0] = y + jnp.dot(h, o_ref[0], preferred_element_type=jnp.float32)


def _s5_scan(ug, wmat, omat, lq_re, lq_im, nc, nb):
    g, r, x = ug.shape
    half = 2 * S5_STATE
    return pl.pallas_call(
        functools.partial(_s5_scan_kernel, nc=nc, nb=nb),
        grid=(g,),
        in_specs=[pl.BlockSpec((1, r, x), lambda i: (i, 0, 0)),
                  pl.BlockSpec((1, x, x + S5_NS), lambda i: (i, 0, 0)),
                  pl.BlockSpec((1, S5_NS, x), lambda i: (i, 0, 0)),
                  pl.BlockSpec((1, 1, half), lambda i: (i, 0, 0)),
                  pl.BlockSpec((1, 1, half), lambda i: (i, 0, 0))],
        out_specs=pl.BlockSpec((1, r, x), lambda i: (i, 0, 0)),
        out_shape=jax.ShapeDtypeStruct((g, r, x), jnp.float32),
        scratch_shapes=[pltpu.VMEM((r, half), jnp.float32)] * 4,
        compiler_params=pltpu.CompilerParams(
            dimension_semantics=("parallel",), vmem_limit_bytes=VMEM_LIMIT_BYTES),
        name="s5_scan",
    )(ug, wmat, omat, lq_re, lq_im)


def _s5_glu_kernel(y_ref, w_ref, b_ref, o_ref):
    y = y_ref[...]
    g = 0.5 * y * (1.0 + jnp.tanh(math.sqrt(2.0 / math.pi) * (y + 0.044715 * (y * y * y))))
    lin = jnp.dot(g.astype(jnp.bfloat16), w_ref[...], preferred_element_type=jnp.float32) + b_ref[...]
    o_ref[...] = g * jax.nn.sigmoid(lin)


def _s5_glu(y, glu_w, glu_b, tm=1024):
    t, d = y.shape
    return pl.pallas_call(
        _s5_glu_kernel,
        grid=(t // tm,),
        in_specs=[pl.BlockSpec((tm, d), lambda i: (i, 0)),
                  pl.BlockSpec((d, d), lambda i: (0, 0)),
                  pl.BlockSpec((1, d), lambda i: (0, 0))],
        out_specs=pl.BlockSpec((tm, d), lambda i: (i, 0)),
        out_shape=jax.ShapeDtypeStruct((t, d), jnp.float32),
        compiler_params=pltpu.CompilerParams(
            dimension_semantics=("parallel",), vmem_limit_bytes=VMEM_LIMIT_BYTES),
        name="s5_glu",
    )(y, glu_w.astype(jnp.bfloat16), glu_b.reshape(1, d).astype(jnp.float32))


def _s5_mixer_pallas(u, ops, glu_w, glu_b):
    bsz, seqlen = u.shape[0], u.shape[1]
    nc = seqlen // S5_Q
    wmat, omat, lq_re, lq_im = ops
    ug = u.astype(jnp.bfloat16).reshape(bsz, nc, S5_Q, S5_GROUPS, S5_GROUP)
    ug = ug.transpose(3, 1, 0, 2, 4).reshape(S5_GROUPS, nc * bsz, S5_X)
    yg = _s5_scan(ug, wmat, omat, lq_re, lq_im, nc, bsz)
    y = yg.reshape(S5_GROUPS, nc, bsz, S5_Q, S5_GROUP).transpose(2, 1, 3, 0, 4).reshape(bsz * seqlen, D_S5)
    return _s5_glu(y, glu_w, glu_b)


NA_RB = 8
NA_BLK = NA_RB * GRID_W
NA_NKEY = NA_KH_MAX * GRID_W
NA_MASKED = -1e30


def _na_bias_table(rpb):
    qc = np.arange(GRID_W)[:, None]
    kc = np.arange(GRID_W)[None, :]
    q_start = np.clip(qc - NA_KW // 2, 0, GRID_W - NA_KW)
    valid = (kc >= q_start) & (kc < q_start + NA_KW)
    dc = np.clip(kc - qc, -(NA_KW - 1), NA_KW - 1) + NA_KW - 1
    dr = np.arange(NA_KH_MAX)[:, None] + np.arange(NA_KH_MAX)[None, :]
    tab = rpb.astype(jnp.float32)[:, dr[:, :, None, None], dc[None, None]]
    tab = jnp.where(jnp.asarray(valid)[None, None, None], tab, NA_MASKED)
    return tab.transpose(1, 0, 3, 2, 4).reshape(NA_KH_MAX, NA_HEADS, GRID_W, NA_NKEY)


def _na_kernel(q_ref, kp_ref, kc_ref, kn_ref, vp_ref, vc_ref, vn_ref, bt_ref, o_ref, kw_ref, vw_ref, *, rows):
    i = pl.program_id(1)
    bf = jnp.bfloat16
    kw_ref[0:NA_BLK] = kp_ref[...].astype(bf)
    kw_ref[NA_BLK:2 * NA_BLK] = kc_ref[...].astype(bf)
    kw_ref[2 * NA_BLK:] = kn_ref[...].astype(bf)
    vw_ref[0:NA_BLK] = vp_ref[...].astype(bf)
    vw_ref[NA_BLK:2 * NA_BLK] = vc_ref[...].astype(bf)
    vw_ref[2 * NA_BLK:] = vn_ref[...].astype(bf)
    lo_lane = lax.broadcasted_iota(jnp.int32, (GRID_W, 2 * NA_HEAD_DIM), 1) < NA_HEAD_DIM
    nt = (((1,), (1,)), ((), ()))
    scale = NA_HEAD_DIM ** -0.5

    def row_body(j, carry):
        r = i * NA_RB + j
        row_start = jnp.clip(r - NA_KH_MAX // 2, 0, rows - NA_KH_MAX)
        e = row_start - r + NA_KH_MAX - 1
        off = pl.multiple_of((row_start - (i - 1) * NA_RB) * GRID_W, GRID_W)
        qoff = pl.multiple_of(j * GRID_W, GRID_W)
        for hp in range(NA_HEADS // 2):
            lanes = slice(hp * 2 * NA_HEAD_DIM, (hp + 1) * 2 * NA_HEAD_DIM)
            qp = (q_ref[pl.ds(qoff, GRID_W), lanes] * scale).astype(bf)
            kpair = kw_ref[pl.ds(off, NA_NKEY), lanes]
            vpair = vw_ref[pl.ds(off, NA_NKEY), lanes]
            outs = []
            for sub in range(2):
                qm = jnp.where(lo_lane if sub == 0 else ~lo_lane, qp, jnp.zeros_like(qp))
                s = lax.dot_general(qm, kpair, nt, preferred_element_type=jnp.float32)
                s = s + bt_ref[e, 2 * hp + sub]
                p = jnp.exp(s - jnp.max(s, axis=-1, keepdims=True))
                den = jnp.sum(p, axis=-1, keepdims=True)
                o = jnp.dot(p.astype(bf), vpair, preferred_element_type=jnp.float32)
                outs.append(o / den)
            o_ref[pl.ds(qoff, GRID_W), lanes] = jnp.where(lo_lane, outs[0], outs[1])
        return carry

    lax.fori_loop(0, NA_RB, row_body, 0)


def _neighbourhood_attention_pallas(pm, bias_tab, bsz, seqlen, qcol):
    rows = seqlen // GRID_W
    nblk = seqlen // NA_BLK
    cur = lambda c: (lambda b, i: (b * nblk + i, c))
    prev = lambda c: (lambda b, i: (b * nblk + jnp.maximum(i - 1, 0), c))
    nxt = lambda c: (lambda b, i: (b * nblk + jnp.minimum(i + 1, nblk - 1), c))
    blk = (NA_BLK, D_NA)
    return pl.pallas_call(
        functools.partial(_na_kernel, rows=rows),
        grid=(bsz, nblk),
        in_specs=[pl.BlockSpec(blk, cur(qcol)),
                  pl.BlockSpec(blk, prev(qcol + 1)), pl.BlockSpec(blk, cur(qcol + 1)), pl.BlockSpec(blk, nxt(qcol + 1)),
                  pl.BlockSpec(blk, prev(qcol + 2)), pl.BlockSpec(blk, cur(qcol + 2)), pl.BlockSpec(blk, nxt(qcol + 2)),
                  pl.BlockSpec(memory_space=pltpu.VMEM)],
        out_specs=pl.BlockSpec(blk, lambda b, i: (b * nblk + i, 0)),
        out_shape=jax.ShapeDtypeStruct((bsz * seqlen, D_NA), jnp.float32),
        scratch_shapes=[pltpu.VMEM((3 * NA_BLK, D_NA), jnp.bfloat16)] * 2,
        compiler_params=pltpu.CompilerParams(
            dimension_semantics=("parallel", "arbitrary"), vmem_limit_bytes=VMEM_LIMIT_BYTES),
        name="natten",
    )(pm, pm, pm, pm, pm, pm, pm, bias_tab)


def _rms_norm(x, g):
    xf = x.astype(jnp.float32)
    y = xf * lax.rsqrt(jnp.mean(xf * xf, axis=-1, keepdims=True) + EPS)
    return (y * g.astype(jnp.float32)).astype(x.dtype)


def _centred_depthwise_conv(x, w, b):
    k = w.shape[0]
    left = k // 2
    y = lax.conv_general_dilated(
        x, w.astype(x.dtype)[:, None, :], window_strides=(1,), padding=[(left, k - 1 - left)],
        dimension_numbers=('NWC', 'WIO', 'NWC'), feature_group_count=x.shape[-1])
    return y + b.astype(x.dtype)


def _ssd_chunked_scan(x, dt, a, bm, cm):
    bsz, seqlen = x.shape[0], x.shape[1]
    nc, q = seqlen // SSD_CHUNK, SSD_CHUNK
    g, j = SSD_GROUPS, SSD_HEADS // SSD_GROUPS
    xr = x.reshape(bsz, nc, q, g, j, SSD_HEAD_DIM)
    dtr = dt.reshape(bsz, nc, q, g, j)
    xdt = xr * dtr[..., None]
    a_cum = jnp.cumsum(dtr * a.reshape(g, j), axis=2)
    br = bm.reshape(bsz, nc, q, g, SSD_STATE)
    cr = cm.reshape(bsz, nc, q, g, SSD_STATE)
    lower_tri = jnp.asarray(np.tril(np.ones((q, q), dtype=bool)))[None, None, :, :, None, None]
    seg = a_cum[:, :, :, None] - a_cum[:, :, None, :]
    decay = jnp.exp(jnp.where(lower_tri, seg, -jnp.inf))
    cb = jnp.einsum('bclgn,bcsgn->bclsg', cr, br)
    y_diag = jnp.einsum('bclsgj,bcsgjp->bclgjp', cb[..., None] * decay, xdt)
    decay_to_end = jnp.exp(a_cum[:, :, -1:] - a_cum)
    chunk_states = jnp.einsum('bcsgn,bcsgjp->bcgjpn', br, xdt * decay_to_end[..., None])
    chunk_decay = jnp.exp(a_cum[:, :, -1])

    def carry_step(h, inp):
        s, d = inp
        return h * d[..., None, None] + s, h

    h0 = jnp.zeros_like(chunk_states[:, 0])
    _, h_prev = lax.scan(carry_step, h0, (jnp.moveaxis(chunk_states, 1, 0), jnp.moveaxis(chunk_decay, 1, 0)))
    h_prev = jnp.moveaxis(h_prev, 0, 1)
    y_off = jnp.einsum('bclgn,bcgjpn->bclgjp', cr, h_prev) * jnp.exp(a_cum)[..., None]
    return (y_diag + y_off).reshape(bsz, seqlen, SSD_HEADS, SSD_HEAD_DIM)


def _ssd_mixer(z, xbc, dt_raw, conv_w, conv_b, dt_bias, a_log, d_skip, norm_g):
    bsz, seqlen = z.shape[0], z.shape[1]
    xbc = jax.nn.silu(_centred_depthwise_conv(xbc, conv_w, conv_b))
    xs, bm, cm = jnp.split(xbc, [D_SSD, D_SSD + SSD_GROUPS * SSD_STATE], axis=-1)
    xs = xs.reshape(bsz, seqlen, SSD_HEADS, SSD_HEAD_DIM)
    bm = bm.reshape(bsz, seqlen, SSD_GROUPS, SSD_STATE)
    cm = cm.reshape(bsz, seqlen, SSD_GROUPS, SSD_STATE)
    dt_bias = dt_bias.astype(jnp.float32)
    a = -jnp.exp(a_log.astype(jnp.float32))
    dt_f = jax.nn.softplus(dt_raw[..., :SSD_HEADS] + dt_bias[0])
    dt_b = jax.nn.softplus(dt_raw[..., SSD_HEADS:] + dt_bias[1])
    flip = lambda t: jnp.flip(t, axis=1)
    y = _ssd_chunked_scan(xs, dt_f, a[0], bm, cm)
    y = y + flip(_ssd_chunked_scan(flip(xs), flip(dt_b), a[1], flip(bm), flip(cm)))
    y = y + xs * d_skip.astype(jnp.float32)[:, None]
    gated = (y.reshape(bsz, seqlen, D_SSD) * jax.nn.silu(z)).reshape(bsz, seqlen, SSD_GROUPS, D_SSD // SSD_GROUPS)
    gated = gated * lax.rsqrt(jnp.mean(gated * gated, axis=-1, keepdims=True) + EPS)
    return gated.reshape(bsz, seqlen, D_SSD) * norm_g.astype(jnp.float32)


def _linear_recurrence_op(e1, e2):
    a1, b1 = e1
    a2, b2 = e2
    return a1 * a2, a2 * b1 + b2


def _s5_mixer(u, a_re, a_im, log_dt, b_re, b_im, c_re, c_im, d_skip, glu_w, glu_b):
    bsz, seqlen = u.shape[0], u.shape[1]
    ug = u.reshape(bsz, seqlen, S5_GROUPS, S5_GROUP)
    ugc = ug.astype(jnp.complex64)
    b_c = lax.complex(b_re.astype(jnp.float32), b_im.astype(jnp.float32))
    y = ug * d_skip.astype(jnp.float32)
    for direction in (0, 1):
        lam = lax.complex(a_re[direction].astype(jnp.float32), a_im[direction].astype(jnp.float32))
        dt = jnp.exp(log_dt[direction].astype(jnp.float32))[:, None]
        lam_bar = jnp.exp(lam * dt)
        b_bar = ((lam_bar - 1.0) / lam)[..., None] * b_c
        bu = jnp.einsum('gph,blgh->blgp', b_bar, ugc)
        lam_seq = jnp.broadcast_to(lam_bar, bu.shape)
        _, h = lax.associative_scan(_linear_recurrence_op, (lam_seq, bu), reverse=(direction == 1), axis=1)
        c_c = lax.complex(c_re[direction].astype(jnp.float32), c_im[direction].astype(jnp.float32))
        y = y + jnp.real(jnp.einsum('ghp,blgp->blgh', c_c, h))
    y = jax.nn.gelu(y.reshape(bsz, seqlen, D_S5))
    return y * jax.nn.sigmoid(y @ glu_w.astype(jnp.float32) + glu_b.astype(jnp.float32))


def _neighbourhood_attention(q, k, v, rpb):
    bsz, seqlen = q.shape[0], q.shape[1]
    rows = seqlen // GRID_W
    kh = min(NA_KH_MAX, rows)
    ncb = GRID_W // NA_QB
    r = np.arange(rows)
    row_start = np.clip(r - kh // 2, 0, rows - kh)
    row_idx = row_start[:, None] + np.arange(kh)
    c0 = np.arange(ncb) * NA_QB
    key_col_start = np.clip(c0 - NA_KW // 2, 0, GRID_W - NA_KB)
    col_idx = key_col_start[:, None] + np.arange(NA_KB)
    q_col = c0[:, None] + np.arange(NA_QB)
    q_col_start = np.clip(q_col - NA_KW // 2, 0, GRID_W - NA_KW)
    valid = (col_idx[:, None, :] >= q_col_start[:, :, None]) & (col_idx[:, None, :] < q_col_start[:, :, None] + NA_KW)
    dc_idx = np.clip(col_idx[:, None, :] - q_col[:, :, None], -(NA_KW - 1), NA_KW - 1) + NA_KW - 1
    dr_idx = row_idx - r[:, None] + NA_KH_MAX - 1

    qg = q.reshape(bsz, rows, ncb, NA_QB, NA_HEADS, NA_HEAD_DIM)
    kgrid = k.reshape(bsz, rows, GRID_W, NA_HEADS, NA_HEAD_DIM)
    vgrid = v.reshape(bsz, rows, GRID_W, NA_HEADS, NA_HEAD_DIM)
    gr = row_idx[:, None, :, None]
    gc = col_idx[None, :, None, :]
    kg = kgrid[:, gr, gc]
    vg = vgrid[:, gr, gc]
    s = jnp.einsum('brcqhd,brckwhd->brchqkw', qg, kg) * (NA_HEAD_DIM ** -0.5)
    bias = rpb.astype(jnp.float32)[:, dr_idx[:, None, None, :, None], dc_idx[None, :, :, None, :]]
    s = s + jnp.moveaxis(bias, 0, 2)[None]
    s = jnp.where(jnp.asarray(valid)[None, None, :, None, :, None, :], s, -jnp.inf)
    p = jax.nn.softmax(s.reshape(s.shape[:-2] + (kh * NA_KB,)), axis=-1).reshape(s.shape)
    o = jnp.einsum('brchqkw,brckwhd->brcqhd', p, vg)
    return o.reshape(bsz, seqlen, D_NA)


def _expert_choice_ffn(xn, router_w, wg, wu, wd):
    n_tok, d = xn.shape
    cap = EC_CAPACITY * n_tok // N_EXPERTS
    aff = jax.nn.softmax(jnp.dot(xn, router_w, precision=lax.Precision.HIGHEST), axis=-1)
    gate, idx = lax.top_k(aff.T, cap)
    xe = xn.astype(jnp.bfloat16)[idx]
    ye = _expert_ffn(xe, wg, wu, wd, gate[..., None])
    out = jnp.zeros((n_tok, d), jnp.float32).at[idx.reshape(-1)].add(ye.reshape(-1, d))
    return out


def _trunk(x, p):
    bsz, seqlen = x.shape[0], x.shape[1]
    n_tok = bsz * seqlen
    x = x.reshape(n_tok, D_MODEL)
    for l in range(DEPTH):
        pm = _norm_matmul(x, p["norm_mix"][l], p["w_in_main"][l])
        dt_raw = _norm_matmul(x, p["norm_mix"][l], p["w_in_dt"][l])[:, :2 * SSD_HEADS]
        pm3 = pm.reshape(bsz, seqlen, -1)
        dt_raw = dt_raw.reshape(bsz, seqlen, -1)
        z = pm3[..., :D_SSD]
        xbc = pm3[..., D_SSD:D_SSD + D_XBC]
        u = pm3[..., D_SSD + D_XBC:D_SSD + D_XBC + D_S5]
        y_ssd = _ssd_mixer(z, xbc, dt_raw, p["conv_w"][l], p["conv_b"][l], p["ssd_dt_bias"][l], p["ssd_a_log"][l],
                           p["ssd_d"][l], p["ssd_norm"][l]).reshape(n_tok, D_SSD)
        y_s5 = _s5_mixer_pallas(u, p["s5_ops"][l], p["s5_glu_w"][l], p["s5_glu_b"][l])
        y_na = _neighbourhood_attention_pallas(pm, p["na_bias"][l], bsz, seqlen, (D_SSD + D_XBC + D_S5) // D_NA)
        x = _out_proj(y_ssd, y_s5, y_na, p["w_out"][l], x)
        xn = _rms_norm(x, p["norm_ffn"][l])
        x = x + _expert_choice_ffn(xn, p["router_w"][l], p["w_gate"][l], p["w_up"][l], p["w_down"][l])
    return _rms_norm(x, p["final_norm"]).reshape(bsz, seqlen, D_MODEL)


def _prepare_params(norm_mix, w_in, conv_w, conv_b, ssd_dt_bias, ssd_a_log, ssd_d, ssd_norm,
                    s5_a_re, s5_a_im, s5_log_dt, s5_b_re, s5_b_im, s5_c_re, s5_c_im, s5_d, s5_glu_w, s5_glu_b,
                    na_rpb, w_out, norm_ffn, router_w, w_gate, w_up, w_down, final_norm):
    bf = jnp.bfloat16
    o0 = D_SSD + D_XBC
    w_in_main = jnp.concatenate([w_in[..., :o0], w_in[..., o0 + 2 * SSD_HEADS:]], axis=-1).astype(bf)
    w_in_dt = jnp.pad(w_in[..., o0:o0 + 2 * SSD_HEADS], ((0, 0), (0, 0), (0, 128 - 2 * SSD_HEADS))).astype(bf)
    s5_ops = [_s5_operators(s5_a_re[l], s5_a_im[l], s5_log_dt[l], s5_b_re[l], s5_b_im[l], s5_c_re[l], s5_c_im[l],
                            s5_d[l]) for l in range(DEPTH)]
    na_bias = [_na_bias_table(na_rpb[l]) for l in range(DEPTH)]
    return dict(norm_mix=norm_mix, w_in_main=w_in_main, w_in_dt=w_in_dt, conv_w=conv_w, conv_b=conv_b,
                ssd_dt_bias=ssd_dt_bias, ssd_a_log=ssd_a_log, ssd_d=ssd_d, ssd_norm=ssd_norm, s5_ops=s5_ops,
                s5_glu_w=s5_glu_w, s5_glu_b=s5_glu_b, na_bias=na_bias, w_out=w_out.astype(bf), norm_ffn=norm_ffn,
                router_w=router_w, w_gate=w_gate.astype(bf), w_up=w_up.astype(bf), w_down=w_down.astype(bf),
                final_norm=final_norm)


def kernel(x_prompt, x_sample, norm_mix, w_in, conv_w, conv_b, ssd_dt_bias, ssd_a_log, ssd_d, ssd_norm,
           s5_a_re, s5_a_im, s5_log_dt, s5_b_re, s5_b_im, s5_c_re, s5_c_im, s5_d, s5_glu_w, s5_glu_b,
           na_rpb, w_out, norm_ffn, router_w, w_gate, w_up, w_down, final_norm):
    p = _prepare_params(norm_mix, w_in, conv_w, conv_b, ssd_dt_bias, ssd_a_log, ssd_d, ssd_norm,
                        s5_a_re, s5_a_im, s5_log_dt, s5_b_re, s5_b_im, s5_c_re, s5_c_im, s5_d, s5_glu_w, s5_glu_b,
                        na_rpb, w_out, norm_ffn, router_w, w_gate, w_up, w_down, final_norm)
    return (_trunk(x_prompt, p), _trunk(x_sample, p))
```

```python
import functools
import math

import jax
import jax.numpy as jnp
import numpy as np
from jax import lax
from jax.experimental import pallas as pl
from jax.experimental.pallas import tpu as pltpu

D_MODEL = 2048
DEPTH = 2
EPS = 1e-6
SSD_HEADS = 16
SSD_HEAD_DIM = 64
D_SSD = SSD_HEADS * SSD_HEAD_DIM
SSD_GROUPS = 2
SSD_STATE = 128
SSD_CONV = 5
SSD_CHUNK = 128
D_XBC = D_SSD + 2 * SSD_GROUPS * SSD_STATE
S5_GROUP = 16
S5_GROUPS = 32
D_S5 = S5_GROUP * S5_GROUPS
S5_STATE = 64
NA_HEADS = 8
NA_HEAD_DIM = 64
D_NA = NA_HEADS * NA_HEAD_DIM
GRID_W = 64
NA_KH_MAX = 8
NA_KW = 16
NA_QB = 16
NA_KB = 32
N_EXPERTS = 16
EXPERT_FF = 2048
EC_CAPACITY = 2

VMEM_LIMIT_BYTES = 56 * 1024 * 1024


def _mm_kernel(x_ref, w_ref, o_ref):
    o_ref[...] = jnp.dot(x_ref[...].astype(jnp.bfloat16), w_ref[...],
                         preferred_element_type=jnp.float32)


def _matmul(x, w, tm=512, tn=512):
    m, k = x.shape
    n = w.shape[1]
    tn = min(tn, n)
    return pl.pallas_call(
        _mm_kernel,
        grid=(m // tm, n // tn),
        in_specs=[pl.BlockSpec((tm, k), lambda i, j: (i, 0)),
                  pl.BlockSpec((k, tn), lambda i, j: (0, j))],
        out_specs=pl.BlockSpec((tm, tn), lambda i, j: (i, j)),
        out_shape=jax.ShapeDtypeStruct((m, n), jnp.float32),
        compiler_params=pltpu.CompilerParams(
            dimension_semantics=("parallel", "arbitrary"),
            vmem_limit_bytes=VMEM_LIMIT_BYTES),
        name="matmul",
    )(x, w)


def _norm_mm_kernel(x_ref, g_ref, w_ref, o_ref, hn_ref):
    @pl.when(pl.program_id(1) == 0)
    def _():
        xf = x_ref[...]
        y = xf * lax.rsqrt(jnp.mean(xf * xf, axis=-1, keepdims=True) + EPS)
        hn_ref[...] = (y * g_ref[...]).astype(jnp.bfloat16)

    o_ref[...] = jnp.dot(hn_ref[...], w_ref[...], preferred_element_type=jnp.float32)


def _norm_matmul(x, g, w, tm=512, tn=512):
    m, k = x.shape
    n = w.shape[1]
    tn = min(tn, n)
    return pl.pallas_call(
        _norm_mm_kernel,
        grid=(m // tm, n // tn),
        in_specs=[pl.BlockSpec((tm, k), lambda i, j: (i, 0)),
                  pl.BlockSpec((1, k), lambda i, j: (0, 0)),
                  pl.BlockSpec((k, tn), lambda i, j: (0, j))],
        out_specs=pl.BlockSpec((tm, tn), lambda i, j: (i, j)),
        out_shape=jax.ShapeDtypeStruct((m, n), jnp.float32),
        scratch_shapes=[pltpu.VMEM((tm, k), jnp.bfloat16)],
        compiler_params=pltpu.CompilerParams(
            dimension_semantics=("parallel", "arbitrary"),
            vmem_limit_bytes=VMEM_LIMIT_BYTES),
        name="norm_matmul",
    )(x, g.reshape(1, k).astype(jnp.float32), w)


def _out_proj_kernel(ya_ref, yb_ref, yc_ref, wa_ref, wb_ref, wc_ref, r_ref, o_ref):
    bf = jnp.bfloat16
    acc = jnp.dot(ya_ref[...].astype(bf), wa_ref[...], preferred_element_type=jnp.float32)
    acc += jnp.dot(yb_ref[...].astype(bf), wb_ref[...], preferred_element_type=jnp.float32)
    acc += jnp.dot(yc_ref[...].astype(bf), wc_ref[...], preferred_element_type=jnp.float32)
    o_ref[...] = r_ref[...] + acc


def _out_proj(y_ssd, y_s5, y_na, w, r, tm=512, tn=512):
    m = r.shape[0]
    n = w.shape[1]
    ka, kb, kc = y_ssd.shape[1], y_s5.shape[1], y_na.shape[1]
    wa, wb, wc = w[:ka], w[ka:ka + kb], w[ka + kb:]
    xspec = lambda k: pl.BlockSpec((tm, k), lambda i, j: (i, 0))
    wspec = lambda k: pl.BlockSpec((k, tn), lambda i, j: (0, j))
    return pl.pallas_call(
        _out_proj_kernel,
        grid=(m // tm, n // tn),
        in_specs=[xspec(ka), xspec(kb), xspec(kc), wspec(ka), wspec(kb), wspec(kc),
                  pl.BlockSpec((tm, tn), lambda i, j: (i, j))],
        out_specs=pl.BlockSpec((tm, tn), lambda i, j: (i, j)),
        out_shape=jax.ShapeDtypeStruct((m, n), jnp.float32),
        compiler_params=pltpu.CompilerParams(
            dimension_semantics=("parallel", "arbitrary"),
            vmem_limit_bytes=VMEM_LIMIT_BYTES),
        name="out_proj",
    )(y_ssd, y_s5, y_na, wa, wb, wc, r)


def _ffn_kernel(x_ref, wg_ref, wu_ref, wd_ref, gate_ref, o_ref, acc_ref):
    f = pl.program_id(2)

    @pl.when(f == 0)
    def _():
        acc_ref[...] = jnp.zeros_like(acc_ref)

    x = x_ref[0]
    hg = jnp.dot(x, wg_ref[0], preferred_element_type=jnp.float32)
    hu = jnp.dot(x, wu_ref[0], preferred_element_type=jnp.float32)
    hid = (hg * jax.nn.sigmoid(hg) * hu).astype(jnp.bfloat16)
    acc_ref[...] += jnp.dot(hid, wd_ref[0], preferred_element_type=jnp.float32)

    @pl.when(f == pl.num_programs(2) - 1)
    def _():
        o_ref[0] = acc_ref[...] * gate_ref[0]


def _expert_ffn(xe, wg, wu, wd, gate, tm=1024, tf=256):
    e, c, d = xe.shape
    f = wg.shape[2]
    return pl.pallas_call(
        _ffn_kernel,
        grid=(e, c // tm, f // tf),
        in_specs=[pl.BlockSpec((1, tm, d), lambda ei, i, j: (ei, i, 0)),
                  pl.BlockSpec((1, d, tf), lambda ei, i, j: (ei, 0, j)),
                  pl.BlockSpec((1, d, tf), lambda ei, i, j: (ei, 0, j)),
                  pl.BlockSpec((1, tf, d), lambda ei, i, j: (ei, j, 0)),
                  pl.BlockSpec((1, tm, 1), lambda ei, i, j: (ei, i, 0))],
        out_specs=pl.BlockSpec((1, tm, d), lambda ei, i, j: (ei, i, 0)),
        out_shape=jax.ShapeDtypeStruct((e, c, d), jnp.float32),
        scratch_shapes=[pltpu.VMEM((tm, d), jnp.float32)],
        compiler_params=pltpu.CompilerParams(
            dimension_semantics=("parallel", "parallel", "arbitrary"),
            vmem_limit_bytes=VMEM_LIMIT_BYTES),
        name="expert_ffn",
    )(xe, wg, wu, wd, gate)


S5_Q = 64
S5_X = S5_Q * S5_GROUP
S5_NS = 4 * S5_STATE


def _s5_operators(a_re, a_im, log_dt, b_re, b_im, c_re, c_im, d_skip):
    f32 = jnp.float32
    hp = lax.Precision.HIGHEST
    q = S5_Q
    lam = lax.complex(a_re.astype(f32), a_im.astype(f32))
    dt = jnp.exp(log_dt.astype(f32))[..., None]
    ldt = lam * dt
    lam_bar = jnp.exp(ldt)
    b_c = lax.complex(b_re.astype(f32), b_im.astype(f32))
    b_bar = ((lam_bar - 1.0) / lam)[..., None] * b_c[None]
    c_c = lax.complex(c_re.astype(f32), c_im.astype(f32))
    steps = jnp.arange(q + 1, dtype=f32)
    pw = jnp.exp(ldt[:, :, None, :] * steps[None, None, :, None])
    kern = jnp.real(jnp.einsum('dgop,dgkp,dgpi->dgkoi', c_c, pw[:, :, :q], b_bar, precision=hp))
    s_idx = np.arange(q)[:, None]
    t_idx = np.arange(q)[None, :]
    diff = t_idx - s_idx
    t_f = kern[0][:, np.maximum(diff, 0)] * jnp.asarray(diff >= 0, f32)[None, :, :, None, None]
    t_b = kern[1][:, np.maximum(-diff, 0)] * jnp.asarray(diff <= 0, f32)[None, :, :, None, None]
    skip = (jnp.asarray(np.eye(q), f32)[None, :, :, None, None]
            * (jnp.eye(S5_GROUP, dtype=f32)[None] * d_skip.astype(f32)[:, None, :])[:, None, None])
    tmat = (t_f + t_b + skip).transpose(0, 1, 4, 2, 3).reshape(S5_GROUPS, S5_X, S5_X)
    p_f = pw[0][:, ::-1][:, 1:, None, :] * jnp.swapaxes(b_bar[0], 1, 2)[:, None]
    p_b = pw[1][:, :q, None, :] * jnp.swapaxes(b_bar[1], 1, 2)[:, None]
    pmat = jnp.concatenate([jnp.real(p_f), jnp.real(p_b), jnp.imag(p_f), jnp.imag(p_b)], axis=-1)
    pmat = pmat.reshape(S5_GROUPS, S5_X, S5_NS)
    o_f = jnp.swapaxes(c_c[0], 1, 2)[:, :, None, :] * jnp.swapaxes(pw[0][:, 1:], 1, 2)[..., None]
    o_b = jnp.swapaxes(c_c[1], 1, 2)[:, :, None, :] * jnp.swapaxes(pw[1][:, ::-1][:, :q], 1, 2)[..., None]
    omat = jnp.concatenate([jnp.real(o_f), jnp.real(o_b), -jnp.imag(o_f), -jnp.imag(o_b)], axis=1)
    omat = omat.reshape(S5_GROUPS, S5_NS, S5_X)
    lam_q = pw[:, :, q]
    lq_re = jnp.concatenate([jnp.real(lam_q[0]), jnp.real(lam_q[1])], axis=-1)[:, None, :]
    lq_im = jnp.concatenate([jnp.imag(lam_q[0]), jnp.imag(lam_q[1])], axis=-1)[:, None, :]
    wmat = jnp.concatenate([tmat, pmat], axis=-1).astype(jnp.bfloat16)
    return wmat, omat.astype(jnp.bfloat16), lq_re, lq_im


def _s5_scan_kernel(u_ref, w_ref, o_ref, lre_ref, lim_ref, y_ref, sre_ref, sim_ref, hre_ref, him_ref, *, nc, nb):
    half = 2 * S5_STATE
    u = u_ref[0]
    s = jnp.dot(u, w_ref[0, :, S5_X:], preferred_element_type=jnp.float32)
    sre_ref[...] = s[:, :half]
    sim_ref[...] = s[:, half:]
    lre = jnp.broadcast_to(lre_ref[0], (nb, half))
    lim = jnp.broadcast_to(lim_ref[0], (nb, half))
    fwd_lane = lax.broadcasted_iota(jnp.int32, (nb, half), 1) < S5_STATE

    def step(c, carry):
        hr, hi = carry
        rf = pl.multiple_of(c * nb, nb)
        rb = pl.multiple_of((nc - 1 - c) * nb, nb)
        hre_ref[pl.ds(rf, nb), :S5_STATE] = hr[:, :S5_STATE]
        him_ref[pl.ds(rf, nb), :S5_STATE] = hi[:, :S5_STATE]
        hre_ref[pl.ds(rb, nb), S5_STATE:] = hr[:, S5_STATE:]
        him_ref[pl.ds(rb, nb), S5_STATE:] = hi[:, S5_STATE:]
        sr = jnp.where(fwd_lane, sre_ref[pl.ds(rf, nb), :], sre_ref[pl.ds(rb, nb), :])
        si = jnp.where(fwd_lane, sim_ref[pl.ds(rf, nb), :], sim_ref[pl.ds(rb, nb), :])
        return lre * hr - lim * hi + sr, lre * hi + lim * hr + si

    zero = jnp.zeros((nb, half), jnp.float32)
    lax.fori_loop(0, nc, step, (zero, zero))
    h = jnp.concatenate([hre_ref[...], him_ref[...]], axis=1).astype(jnp.bfloat16)
    y = jnp.dot(u, w_ref[0, :, :S5_X], preferred_element_type=jnp.float32)
    y_ref[0] = y + jnp.dot(h, o_ref[0], preferred_element_type=jnp.float32)


def _s5_scan(ug, wmat, omat, lq_re, lq_im, nc, nb):
    g, r, x = ug.shape
    half = 2 * S5_STATE
    return pl.pallas_call(
        functools.partial(_s5_scan_kernel, nc=nc, nb=nb),
        grid=(g,),
        in_specs=[pl.BlockSpec((1, r, x), lambda i: (i, 0, 0)),
                  pl.BlockSpec((1, x, x + S5_NS), lambda i: (i, 0, 0)),
                  pl.BlockSpec((1, S5_NS, x), lambda i: (i, 0, 0)),
                  pl.BlockSpec((1, 1, half), lambda i: (i, 0, 0)),
                  pl.BlockSpec((1, 1, half), lambda i: (i, 0, 0))],
        out_specs=pl.BlockSpec((1, r, x), lambda i: (i, 0, 0)),
        out_shape=jax.ShapeDtypeStruct((g, r, x), jnp.float32),
        scratch_shapes=[pltpu.VMEM((r, half), jnp.float32)] * 4,
        compiler_params=pltpu.CompilerParams(
            dimension_semantics=("parallel",), vmem_limit_bytes=VMEM_LIMIT_BYTES),
        name="s5_scan",
    )(ug, wmat, omat, lq_re, lq_im)


def _s5_glu_kernel(y_ref, w_ref, b_ref, o_ref):
    y = y_ref[...]
    g = 0.5 * y * (1.0 + jnp.tanh(math.sqrt(2.0 / math.pi) * (y + 0.044715 * (y * y * y))))
    lin = jnp.dot(g.astype(jnp.bfloat16), w_ref[...], preferred_element_type=jnp.float32) + b_ref[...]
    o_ref[...] = g * jax.nn.sigmoid(lin)


def _s5_glu(y, glu_w, glu_b, tm=1024):
    t, d = y.shape
    return pl.pallas_call(
        _s5_glu_kernel,
        grid=(t // tm,),
        in_specs=[pl.BlockSpec((tm, d), lambda i: (i, 0)),
                  pl.BlockSpec((d, d), lambda i: (0, 0)),
                  pl.BlockSpec((1, d), lambda i: (0, 0))],
        out_specs=pl.BlockSpec((tm, d), lambda i: (i, 0)),
        out_shape=jax.ShapeDtypeStruct((t, d), jnp.float32),
        compiler_params=pltpu.CompilerParams(
            dimension_semantics=("parallel",), vmem_limit_bytes=VMEM_LIMIT_BYTES),
        name="s5_glu",
    )(y, glu_w.astype(jnp.bfloat16), glu_b.reshape(1, d).astype(jnp.float32))


def _s5_mixer_pallas(u, ops, glu_w, glu_b):
    bsz, seqlen = u.shape[0], u.shape[1]
    nc = seqlen // S5_Q
    wmat, omat, lq_re, lq_im = ops
    ug = u.astype(jnp.bfloat16).reshape(bsz, nc, S5_Q, S5_GROUPS, S5_GROUP)
    ug = ug.transpose(3, 1, 0, 2, 4).reshape(S5_GROUPS, nc * bsz, S5_X)
    yg = _s5_scan(ug, wmat, omat, lq_re, lq_im, nc, bsz)
    y = yg.reshape(S5_GROUPS, nc, bsz, S5_Q, S5_GROUP).transpose(2, 1, 3, 0, 4).reshape(bsz * seqlen, D_S5)
    return _s5_glu(y, glu_w, glu_b)


NA_RB = 8
NA_BLK = NA_RB * GRID_W
NA_NKEY = NA_KH_MAX * GRID_W
NA_MASKED = -1e30


def _na_bias_table(rpb):
    qc = np.arange(GRID_W)[:, None]
    kc = np.arange(GRID_W)[None, :]
    q_start = np.clip(qc - NA_KW // 2, 0, GRID_W - NA_KW)
    valid = (kc >= q_start) & (kc < q_start + NA_KW)
    dc = np.clip(kc - qc, -(NA_KW - 1), NA_KW - 1) + NA_KW - 1
    dr = np.arange(NA_KH_MAX)[:, None] + np.arange(NA_KH_MAX)[None, :]
    tab = rpb.astype(jnp.float32)[:, dr[:, :, None, None], dc[None, None]]
    tab = jnp.where(jnp.asarray(valid)[None, None, None], tab, NA_MASKED)
    return tab.transpose(1, 0, 3, 2, 4).reshape(NA_KH_MAX, NA_HEADS, GRID_W, NA_NKEY)


def _na_kernel(q_ref, kp_ref, kc_ref, kn_ref, vp_ref, vc_ref, vn_ref, bt_ref, o_ref, kw_ref, vw_ref, *, rows):
    i = pl.program_id(1)
    bf = jnp.bfloat16
    kw_ref[0:NA_BLK] = kp_ref[...].astype(bf)
    kw_ref[NA_BLK:2 * NA_BLK] = kc_ref[...].astype(bf)
    kw_ref[2 * NA_BLK:] = kn_ref[...].astype(bf)
    vw_ref[0:NA_BLK] = vp_ref[...].astype(bf)
    vw_ref[NA_BLK:2 * NA_BLK] = vc_ref[...].astype(bf)
    vw_ref[2 * NA_BLK:] = vn_ref[...].astype(bf)
    lo_lane = lax.broadcasted_iota(jnp.int32, (GRID_W, 2 * NA_HEAD_DIM), 1) < NA_HEAD_DIM
    nt = (((1,), (1,)), ((), ()))
    scale = NA_HEAD_DIM ** -0.5

    def row_body(j, carry):
        r = i * NA_RB + j
        row_start = jnp.clip(r - NA_KH_MAX // 2, 0, rows - NA_KH_MAX)
        e = row_start - r + NA_KH_MAX - 1
        off = pl.multiple_of((row_start - (i - 1) * NA_RB) * GRID_W, GRID_W)
        qoff = pl.multiple_of(j * GRID_W, GRID_W)
        for hp in range(NA_HEADS // 2):
            lanes = slice(hp * 2 * NA_HEAD_DIM, (hp + 1) * 2 * NA_HEAD_DIM)
            qp = (q_ref[pl.ds(qoff, GRID_W), lanes] * scale).astype(bf)
            kpair = kw_ref[pl.ds(off, NA_NKEY), lanes]
            vpair = vw_ref[pl.ds(off, NA_NKEY), lanes]
            outs = []
            for sub in range(2):
                qm = jnp.where(lo_lane if sub == 0 else ~lo_lane, qp, jnp.zeros_like(qp))
                s = lax.dot_general(qm, kpair, nt, preferred_element_type=jnp.float32)
                s = s + bt_ref[e, 2 * hp + sub]
                p = jnp.exp(s - jnp.max(s, axis=-1, keepdims=True))
                den = jnp.sum(p, axis=-1, keepdims=True)
                o = jnp.dot(p.astype(bf), vpair, preferred_element_type=jnp.float32)
                outs.append(o / den)
            o_ref[pl.ds(qoff, GRID_W), lanes] = jnp.where(lo_lane, outs[0], outs[1])
        return carry

    lax.fori_loop(0, NA_RB, row_body, 0)


def _neighbourhood_attention_pallas(pm, bias_tab, bsz, seqlen, qcol):
    rows = seqlen // GRID_W
    nblk = seqlen // NA_BLK
    cur = lambda c: (lambda b, i: (b * nblk + i, c))
    prev = lambda c: (lambda b, i: (b * nblk + jnp.maximum(i - 1, 0), c))
    nxt = lambda c: (lambda b, i: (b * nblk + jnp.minimum(i + 1, nblk - 1), c))
    blk = (NA_BLK, D_NA)
    return pl.pallas_call(
        functools.partial(_na_kernel, rows=rows),
        grid=(bsz, nblk),
        in_specs=[pl.BlockSpec(blk, cur(qcol)),
                  pl.BlockSpec(blk, prev(qcol + 1)), pl.BlockSpec(blk, cur(qcol + 1)), pl.BlockSpec(blk, nxt(qcol + 1)),
                  pl.BlockSpec(blk, prev(qcol + 2)), pl.BlockSpec(blk, cur(qcol + 2)), pl.BlockSpec(blk, nxt(qcol + 2)),
                  pl.BlockSpec(memory_space=pltpu.VMEM)],
        out_specs=pl.BlockSpec(blk, lambda b, i: (b * nblk + i, 0)),
        out_shape=jax.ShapeDtypeStruct((bsz * seqlen, D_NA), jnp.float32),
        scratch_shapes=[pltpu.VMEM((3 * NA_BLK, D_NA), jnp.bfloat16)] * 2,
        compiler_params=pltpu.CompilerParams(
            dimension_semantics=("parallel", "arbitrary"), vmem_limit_bytes=VMEM_LIMIT_BYTES),
        name="natten",
    )(pm, pm, pm, pm, pm, pm, pm, bias_tab)


CONV_ROWS = 512
CONV_HALO = 8
DT_LANES = 128
SSD_HP = 2 * SSD_HEAD_DIM
SSD_GW = D_SSD // SSD_GROUPS


def _conv_silu_kernel(prev_ref, cur_ref, next_ref, w_ref, b_ref, o_ref, *, nblk):
    i = pl.program_id(1)
    prev = jnp.where(i > 0, prev_ref[...], 0.0)
    nxt = jnp.where(i < nblk - 1, next_ref[...], 0.0)
    xe = jnp.concatenate([prev, cur_ref[...], nxt], axis=0)
    base = CONV_HALO - SSD_CONV // 2
    acc = b_ref[...] + w_ref[0:1, :] * xe[base:base + CONV_ROWS]
    for k in range(1, SSD_CONV):
        acc = acc + w_ref[k:k + 1, :] * xe[base + k:base + k + CONV_ROWS]
    o_ref[...] = acc * jax.nn.sigmoid(acc)


def _conv_silu(pm, conv_w, conv_b, bsz, seqlen, col0):
    nblk = seqlen // CONV_ROWS
    hb = CONV_ROWS // CONV_HALO
    ncol = D_XBC // 512
    cur = pl.BlockSpec((CONV_ROWS, 512), lambda b, i, j: (b * nblk + i, col0 + j))
    prev = pl.BlockSpec((CONV_HALO, 512), lambda b, i, j: (jnp.maximum((b * nblk + i) * hb - 1, 0), col0 + j))
    nxt = pl.BlockSpec((CONV_HALO, 512),
                       lambda b, i, j: (jnp.minimum((b * nblk + i + 1) * hb, bsz * nblk * hb - 1), col0 + j))
    return pl.pallas_call(
        functools.partial(_conv_silu_kernel, nblk=nblk),
        grid=(bsz, nblk, ncol),
        in_specs=[prev, cur, nxt,
                  pl.BlockSpec((SSD_CONV, 512), lambda b, i, j: (0, j)),
                  pl.BlockSpec((1, 512), lambda b, i, j: (0, j))],
        out_specs=pl.BlockSpec((CONV_ROWS, 512), lambda b, i, j: (b * nblk + i, j)),
        out_shape=jax.ShapeDtypeStruct((bsz * seqlen, D_XBC), jnp.float32),
        compiler_params=pltpu.CompilerParams(
            dimension_semantics=("parallel", "parallel", "parallel"), vmem_limit_bytes=VMEM_LIMIT_BYTES),
        name="conv_silu",
    )(pm, pm, pm, conv_w.astype(jnp.float32), conv_b.reshape(1, D_XBC).astype(jnp.float32))


def _split_bf16(v, terms):
    out = []
    for _ in range(terms):
        t = v.astype(jnp.bfloat16)
        out.append(t)
        v = v - t.astype(jnp.float32)
    return out


def _ssd_sweep_kernel(*refs, reverse, final):
    if final:
        xc_ref, dt_ref, bias_ref, a_ref, e_ref, tri_ref, yprev_ref, z_ref, dsk_ref, g_ref, o_ref, h_ref = refs
    else:
        xc_ref, dt_ref, bias_ref, a_ref, e_ref, tri_ref, o_ref, h_ref = refs
    f32, bf = jnp.float32, jnp.bfloat16
    q = SSD_CHUNK
    diroff = SSD_HEADS if reverse else 0
    end = 0 if reverse else q - 1

    @pl.when(pl.program_id(1) == 0)
    def _():
        h_ref[...] = jnp.zeros_like(h_ref)

    xs = xc_ref[:, :D_SSD]
    dtr = dt_ref[...] + bias_ref[...]
    dt = jnp.maximum(dtr, 0.0) + jnp.log1p(jnp.exp(-jnp.abs(dtr)))
    da = dt * a_ref[...]
    tri = tri_ref[...]
    a_cum = sum(jnp.dot(tri, t, preferred_element_type=f32) for t in _split_bf16(da, 3))
    a_end = a_cum[end:end + 1, :]

    def expand(v):
        return jnp.dot(jnp.concatenate(_split_bf16(v, 2), axis=1), e_ref[...], preferred_element_type=f32)

    dt_x = expand(dt)
    eac_x = expand(jnp.exp(a_cum))
    dte_x = expand(jnp.exp(a_end - a_cum))
    xdt = xs * dt_x
    xdt_bf = xdt.astype(bf)
    xw_bf = (xdt * dte_x).astype(bf)

    a_cum_t = a_cum.T
    li = lax.broadcasted_iota(jnp.int32, (q, q), 0)
    si = lax.broadcasted_iota(jnp.int32, (q, q), 1)
    causal = (li <= si) if reverse else (li >= si)
    lo_lane = lax.broadcasted_iota(jnp.int32, (q, SSD_HP), 1) < SSD_HEAD_DIM
    heads_per_group = SSD_HEADS // SSD_GROUPS

    for g in range(SSD_GROUPS):
        bm = xc_ref[:, D_SSD + g * SSD_STATE:D_SSD + (g + 1) * SSD_STATE]
        cm = xc_ref[:, D_SSD + (SSD_GROUPS + g) * SSD_STATE:D_SSD + (SSD_GROUPS + g + 1) * SSD_STATE].astype(bf)
        bm_t = bm.T.astype(bf)
        cb = jnp.dot(cm, bm_t, preferred_element_type=f32)
        gl = slice(g * SSD_GW, (g + 1) * SSD_GW)
        h_old = h_ref[g]
        y_off = jnp.dot(cm, h_old.astype(bf), preferred_element_type=f32) * eac_x[:, gl]
        h_ref[g] = h_old * eac_x[end:end + 1, gl] + jnp.dot(bm_t, xw_bf[:, gl], preferred_element_type=f32)
        for kp in range(heads_per_group // 2):
            pair = g * (heads_per_group // 2) + kp
            pl_ = slice(pair * SSD_HP, (pair + 1) * SSD_HP)
            halves = []
            for sub in range(2):
                hl = diroff + 2 * pair + sub
                seg = a_cum[:, hl:hl + 1] - a_cum_t[hl:hl + 1, :]
                m = (cb * jnp.exp(jnp.where(causal, seg, NA_MASKED))).astype(bf)
                halves.append(jnp.dot(m, xdt_bf[:, pl_], preferred_element_type=f32))
            y = jnp.where(lo_lane, halves[0], halves[1]) + y_off[:, kp * SSD_HP:(kp + 1) * SSD_HP]
            if final:
                y = y + yprev_ref[:, pl_] + xs[:, pl_] * dsk_ref[:, pl_]
                zz = z_ref[:, pl_]
                y = y * (zz * jax.nn.sigmoid(zz))
            o_ref[:, pl_] = y

    if final:
        for g in range(SSD_GROUPS):
            gl = slice(g * SSD_GW, (g + 1) * SSD_GW)
            gated = o_ref[:, gl]
            rs = lax.rsqrt(jnp.mean(gated * gated, axis=-1, keepdims=True) + EPS)
            o_ref[:, gl] = gated * rs * g_ref[:, gl]


def _ssd_expand_matrix(reverse):
    e = np.zeros((2 * DT_LANES, D_SSD), np.float32)
    off = SSD_HEADS if reverse else 0
    for h in range(SSD_HEADS):
        for term in range(2):
            e[term * DT_LANES + off + h, h * SSD_HEAD_DIM:(h + 1) * SSD_HEAD_DIM] = 1.0
    return jnp.asarray(e, jnp.bfloat16)


def _ssd_sweep(xc, dt_raw, bias_all, a_all, bsz, seqlen, reverse, final_args=None):
    nc = seqlen // SSD_CHUNK
    q = SSD_CHUNK
    final = final_args is not None
    tri_np = np.triu(np.ones((q, q), np.float32)) if reverse else np.tril(np.ones((q, q), np.float32))
    row = (lambda b, c: (b * nc + nc - 1 - c, 0)) if reverse else (lambda b, c: (b * nc + c, 0))
    const = lambda b, c: (0, 0)
    in_specs = [pl.BlockSpec((q, D_XBC), row),
                pl.BlockSpec((q, DT_LANES), row),
                pl.BlockSpec((1, DT_LANES), const),
                pl.BlockSpec((1, DT_LANES), const),
                pl.BlockSpec((2 * DT_LANES, D_SSD), const),
                pl.BlockSpec((q, q), const)]
    args = [xc, dt_raw, bias_all, a_all, _ssd_expand_matrix(reverse), jnp.asarray(tri_np, jnp.bfloat16)]
    if final:
        y_prev, pm, d_skip_x, norm_g = final_args
        in_specs += [pl.BlockSpec((q, D_SSD), row),
                     pl.BlockSpec((q, D_SSD), row),
                     pl.BlockSpec((1, D_SSD), const),
                     pl.BlockSpec((1, D_SSD), const)]
        args += [y_prev, pm, d_skip_x, norm_g]
    return pl.pallas_call(
        functools.partial(_ssd_sweep_kernel, reverse=reverse, final=final),
        grid=(bsz, nc),
        in_specs=in_specs,
        out_specs=pl.BlockSpec((q, D_SSD), row),
        out_shape=jax.ShapeDtypeStruct((bsz * seqlen, D_SSD), jnp.float32),
        scratch_shapes=[pltpu.VMEM((SSD_GROUPS, SSD_STATE, SSD_GW), jnp.float32)],
        compiler_params=pltpu.CompilerParams(
            dimension_semantics=("parallel", "arbitrary"), vmem_limit_bytes=VMEM_LIMIT_BYTES),
        name="ssd_sweep_bwd" if reverse else "ssd_sweep_fwd",
    )(*args)


def _ssd_mixer_pallas(pm, dt_raw, sp, bsz, seqlen):
    xc = _conv_silu(pm, sp["conv_w"], sp["conv_b"], bsz, seqlen, D_SSD // 512)
    y_f = _ssd_sweep(xc, dt_raw, sp["dt_bias"], sp["a"], bsz, seqlen, reverse=False)
    return _ssd_sweep(xc, dt_raw, sp["dt_bias"], sp["a"], bsz, seqlen, reverse=True,
                      final_args=(y_f, pm, sp["d_skip"], sp["norm_g"]))


def _ssd_params(conv_w, conv_b, dt_bias, a_log, d_skip, norm_g):
    f32 = jnp.float32
    pad = DT_LANES - 2 * SSD_HEADS
    return dict(conv_w=conv_w, conv_b=conv_b,
                dt_bias=jnp.pad(dt_bias.astype(f32).reshape(1, -1), ((0, 0), (0, pad))),
                a=jnp.pad(-jnp.exp(a_log.astype(f32)).reshape(1, -1), ((0, 0), (0, pad))),
                d_skip=jnp.repeat(d_skip.astype(f32), SSD_HEAD_DIM).reshape(1, D_SSD),
                norm_g=norm_g.astype(f32).reshape(1, D_SSD))


LANES = 128
ROUTER_TM = 1024
SEL_PC = 512


def _router_kernel(x_ref, g_ref, w_ref, xn_ref, aff_ref):
    f32 = jnp.float32
    xf = x_ref[...]
    xn = xf * lax.rsqrt(jnp.mean(xf * xf, axis=-1, keepdims=True) + EPS) * g_ref[...]
    xn_ref[...] = xn.astype(jnp.bfloat16)
    hi, mid = _split_bf16(xn, 2)
    p_hi = jnp.dot(hi, w_ref[...], preferred_element_type=f32)
    p_mid = jnp.dot(mid, w_ref[...], preferred_element_type=f32)
    ne = N_EXPERTS
    second = pltpu.roll(p_hi, LANES - ne, axis=1) + p_mid
    third = pltpu.roll(p_hi, LANES - 2 * ne, axis=1) + pltpu.roll(p_mid, LANES - ne, axis=1)
    logits = (p_hi + (second + third)).T[0:ne]
    ex = jnp.exp(logits - jnp.max(logits, axis=0, keepdims=True))
    aff = ex / jnp.sum(ex, axis=0, keepdims=True)
    for jb in range(ROUTER_TM // LANES):
        aff_ref[jb * ne:(jb + 1) * ne, :] = aff[:, jb * LANES:(jb + 1) * LANES]


def _router(x, g, router_w):
    t, d = x.shape
    ne = N_EXPERTS
    w_terms = _split_bf16(router_w.astype(jnp.float32), 3)
    wst = jnp.concatenate(w_terms + [jnp.zeros((d, LANES - 3 * ne), jnp.bfloat16)], axis=1)
    rows = ROUTER_TM // LANES * ne
    return pl.pallas_call(
        _router_kernel,
        grid=(t // ROUTER_TM,),
        in_specs=[pl.BlockSpec((ROUTER_TM, d), lambda i: (i, 0)),
                  pl.BlockSpec((1, d), lambda i: (0, 0)),
                  pl.BlockSpec((d, LANES), lambda i: (0, 0))],
        out_specs=[pl.BlockSpec((ROUTER_TM, d), lambda i: (i, 0)),
                   pl.BlockSpec((rows, LANES), lambda i: (i, 0))],
        out_shape=[jax.ShapeDtypeStruct((t, d), jnp.bfloat16),
                   jax.ShapeDtypeStruct((t // LANES * ne, LANES), jnp.float32)],
        compiler_params=pltpu.CompilerParams(
            dimension_semantics=("parallel",), vmem_limit_bytes=VMEM_LIMIT_BYTES),
        name="router",
    )(x, g.reshape(1, d).astype(jnp.float32), wst)


def _select_kernel(aff_ref, idx_ref, gate_ref, gt_ref, eq_ref, need_ref, *, nb, cap):
    f32, bf, i32 = jnp.float32, jnp.bfloat16, jnp.int32
    ne = N_EXPERTS
    bits3 = pltpu.bitcast(aff_ref[...], i32).reshape(nb, ne, LANES)

    def count_ge(v):
        c = jnp.sum((bits3 >= v[None]).astype(i32), axis=0)
        return jnp.broadcast_to(jnp.sum(c, axis=1, keepdims=True), (ne, LANES))

    def bisect(_, carry):
        lo, hi = carry
        mid = lo + lax.shift_right_logical(hi - lo, 1)
        ok = count_ge(mid) >= cap
        return jnp.where(ok, mid, lo), jnp.where(ok, hi, mid)

    lo0 = jnp.zeros((ne, LANES), i32)
    hi0 = jnp.full((ne, LANES), 0x7F800000, i32)
    thr, _ = lax.fori_loop(0, 31, bisect, (lo0, hi0))
    gt3 = bits3 > thr[None]
    gt_ref[...] = gt3.astype(f32).reshape(nb * ne, LANES)
    eq_ref[...] = (bits3 == thr[None]).astype(f32).reshape(nb * ne, LANES)
    n_gt = jnp.sum(jnp.sum(gt3.astype(i32), axis=0), axis=1, keepdims=True)
    need_ref[...] = jnp.broadcast_to((cap - n_gt).astype(f32), (ne, LANES))

    r_i = lax.broadcasted_iota(i32, (LANES, LANES), 0)
    c_i = lax.broadcasted_iota(i32, (LANES, LANES), 1)
    tri_incl = (r_i <= c_i).astype(bf)
    tri_incl_t = (c_i <= r_i).astype(bf)
    ones_sq = jnp.ones((LANES, LANES), bf)
    rb = lax.broadcasted_iota(i32, (nb, nb), 0)
    cb = lax.broadcasted_iota(i32, (nb, nb), 1)
    below = (cb < rb).astype(bf)
    sub_nb = lax.broadcasted_iota(i32, (nb, SEL_PC), 0)
    sub_128 = lax.broadcasted_iota(i32, (LANES, SEL_PC), 0)
    slot_local = lax.broadcasted_iota(i32, (1, SEL_PC), 1)
    reps = SEL_PC // LANES

    def rank_before(m):
        incl = jnp.dot(m, tri_incl, preferred_element_type=f32)
        tot = jnp.dot(m, ones_sq, preferred_element_type=f32)
        off = jnp.dot(below, tot.astype(bf), preferred_element_type=f32)
        return incl, tot, off

    def per_expert(e, carry):
        rows_e = pl.ds(e, nb, stride=ne)
        gt = gt_ref[rows_e, :]
        eq = eq_ref[rows_e, :]
        eq_incl, _, eq_off = rank_before(eq.astype(bf))
        need = need_ref[pl.ds(e, 1), :]
        sel = jnp.maximum(gt, jnp.where(eq_off + eq_incl - eq < need, eq, 0.0))
        _, tot, off = rank_before(sel.astype(bf))
        tot_b = jnp.concatenate([tot] * reps, axis=1)
        cum_b = jnp.concatenate([off + tot] * reps, axis=1)
        incl_t = jnp.dot(tri_incl_t, sel.T.astype(bf), preferred_element_type=f32)
        table_t = jnp.concatenate([incl_t.astype(bf)] + _split_bf16(aff_ref[rows_e, :].T, 3), axis=0)
        for pc in range(cap // SEL_PC):
            slot = (slot_local + pc * SEL_PC).astype(f32)
            done = cum_b <= slot
            row = jnp.sum(done.astype(i32), axis=0, keepdims=True)
            before = jnp.sum(jnp.where(done, tot_b, 0.0), axis=0, keepdims=True)
            onehot_t = (sub_nb == row).astype(bf)
            got = jnp.dot(table_t, onehot_t, preferred_element_type=f32)
            lane = jnp.sum((got[:LANES] <= slot - before).astype(i32), axis=0, keepdims=True)
            aff_row = got[LANES:2 * LANES] + (got[2 * LANES:3 * LANES] + got[3 * LANES:])
            cols = slice(pc * SEL_PC, (pc + 1) * SEL_PC)
            idx_ref[pl.ds(e, 1), cols] = row * LANES + lane
            gate_ref[pl.ds(e, 1), cols] = jnp.sum(jnp.where(sub_128 == lane, aff_row, 0.0), axis=0, keepdims=True)
        return carry

    lax.fori_loop(0, ne, per_expert, 0)


def _select(aff, n_tok):
    nb = n_tok // LANES
    cap = EC_CAPACITY * n_tok // N_EXPERTS
    vm = pl.BlockSpec(memory_space=pltpu.VMEM)
    return pl.pallas_call(
        functools.partial(_select_kernel, nb=nb, cap=cap),
        in_specs=[vm],
        out_specs=[vm, vm],
        out_shape=[jax.ShapeDtypeStruct((N_EXPERTS, cap), jnp.int32),
                   jax.ShapeDtypeStruct((N_EXPERTS, cap), jnp.float32)],
        scratch_shapes=[pltpu.VMEM((nb * N_EXPERTS, LANES), jnp.float32),
                        pltpu.VMEM((nb * N_EXPERTS, LANES), jnp.float32),
                        pltpu.VMEM((N_EXPERTS, LANES), jnp.float32)],
        compiler_params=pltpu.CompilerParams(vmem_limit_bytes=VMEM_LIMIT_BYTES),
        name="select",
    )(aff)


def _rms_norm(x, g):
    xf = x.astype(jnp.float32)
    y = xf * lax.rsqrt(jnp.mean(xf * xf, axis=-1, keepdims=True) + EPS)
    return (y * g.astype(jnp.float32)).astype(x.dtype)


def _centred_depthwise_conv(x, w, b):
    k = w.shape[0]
    left = k // 2
    y = lax.conv_general_dilated(
        x, w.astype(x.dtype)[:, None, :], window_strides=(1,), padding=[(left, k - 1 - left)],
        dimension_numbers=('NWC', 'WIO', 'NWC'), feature_group_count=x.shape[-1])
    return y + b.astype(x.dtype)


def _ssd_chunked_scan(x, dt, a, bm, cm):
    bsz, seqlen = x.shape[0], x.shape[1]
    nc, q = seqlen // SSD_CHUNK, SSD_CHUNK
    g, j = SSD_GROUPS, SSD_HEADS // SSD_GROUPS
    xr = x.reshape(bsz, nc, q, g, j, SSD_HEAD_DIM)
    dtr = dt.reshape(bsz, nc, q, g, j)
    xdt = xr * dtr[..., None]
    a_cum = jnp.cumsum(dtr * a.reshape(g, j), axis=2)
    br = bm.reshape(bsz, nc, q, g, SSD_STATE)
    cr = cm.reshape(bsz, nc, q, g, SSD_STATE)
    lower_tri = jnp.asarray(np.tril(np.ones((q, q), dtype=bool)))[None, None, :, :, None, None]
    seg = a_cum[:, :, :, None] - a_cum[:, :, None, :]
    decay = jnp.exp(jnp.where(lower_tri, seg, -jnp.inf))
    cb = jnp.einsum('bclgn,bcsgn->bclsg', cr, br)
    y_diag = jnp.einsum('bclsgj,bcsgjp->bclgjp', cb[..., None] * decay, xdt)
    decay_to_end = jnp.exp(a_cum[:, :, -1:] - a_cum)
    chunk_states = jnp.einsum('bcsgn,bcsgjp->bcgjpn', br, xdt * decay_to_end[..., None])
    chunk_decay = jnp.exp(a_cum[:, :, -1])

    def carry_step(h, inp):
        s, d = inp
        return h * d[..., None, None] + s, h

    h0 = jnp.zeros_like(chunk_states[:, 0])
    _, h_prev = lax.scan(carry_step, h0, (jnp.moveaxis(chunk_states, 1, 0), jnp.moveaxis(chunk_decay, 1, 0)))
    h_prev = jnp.moveaxis(h_prev, 0, 1)
    y_off = jnp.einsum('bclgn,bcgjpn->bclgjp', cr, h_prev) * jnp.exp(a_cum)[..., None]
    return (y_diag + y_off).reshape(bsz, seqlen, SSD_HEADS, SSD_HEAD_DIM)


def _ssd_mixer(z, xbc, dt_raw, conv_w, conv_b, dt_bias, a_log, d_skip, norm_g):
    bsz, seqlen = z.shape[0], z.shape[1]
    xbc = jax.nn.silu(_centred_depthwise_conv(xbc, conv_w, conv_b))
    xs, bm, cm = jnp.split(xbc, [D_SSD, D_SSD + SSD_GROUPS * SSD_STATE], axis=-1)
    xs = xs.reshape(bsz, seqlen, SSD_HEADS, SSD_HEAD_DIM)
    bm = bm.reshape(bsz, seqlen, SSD_GROUPS, SSD_STATE)
    cm = cm.reshape(bsz, seqlen, SSD_GROUPS, SSD_STATE)
    dt_bias = dt_bias.astype(jnp.float32)
    a = -jnp.exp(a_log.astype(jnp.float32))
    dt_f = jax.nn.softplus(dt_raw[..., :SSD_HEADS] + dt_bias[0])
    dt_b = jax.nn.softplus(dt_raw[..., SSD_HEADS:] + dt_bias[1])
    flip = lambda t: jnp.flip(t, axis=1)
    y = _ssd_chunked_scan(xs, dt_f, a[0], bm, cm)
    y = y + flip(_ssd_chunked_scan(flip(xs), flip(dt_b), a[1], flip(bm), flip(cm)))
    y = y + xs * d_skip.astype(jnp.float32)[:, None]
    gated = (y.reshape(bsz, seqlen, D_SSD) * jax.nn.silu(z)).reshape(bsz, seqlen, SSD_GROUPS, D_SSD // SSD_GROUPS)
    gated = gated * lax.rsqrt(jnp.mean(gated * gated, axis=-1, keepdims=True) + EPS)
    return gated.reshape(bsz, seqlen, D_SSD) * norm_g.astype(jnp.float32)


def _linear_recurrence_op(e1, e2):
    a1, b1 = e1
    a2, b2 = e2
    return a1 * a2, a2 * b1 + b2


def _s5_mixer(u, a_re, a_im, log_dt, b_re, b_im, c_re, c_im, d_skip, glu_w, glu_b):
    bsz, seqlen = u.shape[0], u.shape[1]
    ug = u.reshape(bsz, seqlen, S5_GROUPS, S5_GROUP)
    ugc = ug.astype(jnp.complex64)
    b_c = lax.complex(b_re.astype(jnp.float32), b_im.astype(jnp.float32))
    y = ug * d_skip.astype(jnp.float32)
    for direction in (0, 1):
        lam = lax.complex(a_re[direction].astype(jnp.float32), a_im[direction].astype(jnp.float32))
        dt = jnp.exp(log_dt[direction].astype(jnp.float32))[:, None]
        lam_bar = jnp.exp(lam * dt)
        b_bar = ((lam_bar - 1.0) / lam)[..., None] * b_c
        bu = jnp.einsum('gph,blgh->blgp', b_bar, ugc)
        lam_seq = jnp.broadcast_to(lam_bar, bu.shape)
        _, h = lax.associative_scan(_linear_recurrence_op, (lam_seq, bu), reverse=(direction == 1), axis=1)
        c_c = lax.complex(c_re[direction].astype(jnp.float32), c_im[direction].astype(jnp.float32))
        y = y + jnp.real(jnp.einsum('ghp,blgp->blgh', c_c, h))
    y = jax.nn.gelu(y.reshape(bsz, seqlen, D_S5))
    return y * jax.nn.sigmoid(y @ glu_w.astype(jnp.float32) + glu_b.astype(jnp.float32))


def _neighbourhood_attention(q, k, v, rpb):
    bsz, seqlen = q.shape[0], q.shape[1]
    rows = seqlen // GRID_W
    kh = min(NA_KH_MAX, rows)
    ncb = GRID_W // NA_QB
    r = np.arange(rows)
    row_start = np.clip(r - kh // 2, 0, rows - kh)
    row_idx = row_start[:, None] + np.arange(kh)
    c0 = np.arange(ncb) * NA_QB
    key_col_start = np.clip(c0 - NA_KW // 2, 0, GRID_W - NA_KB)
    col_idx = key_col_start[:, None] + np.arange(NA_KB)
    q_col = c0[:, None] + np.arange(NA_QB)
    q_col_start = np.clip(q_col - NA_KW // 2, 0, GRID_W - NA_KW)
    valid = (col_idx[:, None, :] >= q_col_start[:, :, None]) & (col_idx[:, None, :] < q_col_start[:, :, None] + NA_KW)
    dc_idx = np.clip(col_idx[:, None, :] - q_col[:, :, None], -(NA_KW - 1), NA_KW - 1) + NA_KW - 1
    dr_idx = row_idx - r[:, None] + NA_KH_MAX - 1

    qg = q.reshape(bsz, rows, ncb, NA_QB, NA_HEADS, NA_HEAD_DIM)
    kgrid = k.reshape(bsz, rows, GRID_W, NA_HEADS, NA_HEAD_DIM)
    vgrid = v.reshape(bsz, rows, GRID_W, NA_HEADS, NA_HEAD_DIM)
    gr = row_idx[:, None, :, None]
    gc = col_idx[None, :, None, :]
    kg = kgrid[:, gr, gc]
    vg = vgrid[:, gr, gc]
    s = jnp.einsum('brcqhd,brckwhd->brchqkw', qg, kg) * (NA_HEAD_DIM ** -0.5)
    bias = rpb.astype(jnp.float32)[:, dr_idx[:, None, None, :, None], dc_idx[None, :, :, None, :]]
    s = s + jnp.moveaxis(bias, 0, 2)[None]
    s = jnp.where(jnp.asarray(valid)[None, None, :, None, :, None, :], s, -jnp.inf)
    p = jax.nn.softmax(s.reshape(s.shape[:-2] + (kh * NA_KB,)), axis=-1).reshape(s.shape)
    o = jnp.einsum('brchqkw,brckwhd->brcqhd', p, vg)
    return o.reshape(bsz, seqlen, D_NA)


def _expert_choice_ffn(x, norm_g, router_w, wg, wu, wd):
    n_tok, d = x.shape
    xn, aff = _router(x, norm_g, router_w)
    idx, gate = _select(aff, n_tok)
    xe = xn[idx]
    ye = _expert_ffn(xe, wg, wu, wd, gate[..., None])
    return x.at[idx.reshape(-1)].add(ye.reshape(-1, d))


def _trunk(x, p):
    bsz, seqlen = x.shape[0], x.shape[1]
    n_tok = bsz * seqlen
    x = x.reshape(n_tok, D_MODEL)
    for l in range(DEPTH):
        pm = _norm_matmul(x, p["norm_mix"][l], p["w_in_main"][l])
        dt_raw = _norm_matmul(x, p["norm_mix"][l], p["w_in_dt"][l])
        pm3 = pm.reshape(bsz, seqlen, -1)
        u = pm3[..., D_SSD + D_XBC:D_SSD + D_XBC + D_S5]
        y_ssd = _ssd_mixer_pallas(pm, dt_raw, p["ssd"][l], bsz, seqlen)
        y_s5 = _s5_mixer_pallas(u, p["s5_ops"][l], p["s5_glu_w"][l], p["s5_glu_b"][l])
        y_na = _neighbourhood_attention_pallas(pm, p["na_bias"][l], bsz, seqlen, (D_SSD + D_XBC + D_S5) // D_NA)
        x = _out_proj(y_ssd, y_s5, y_na, p["w_out"][l], x)
        x = _expert_choice_ffn(x, p["norm_ffn"][l], p["router_w"][l], p["w_gate"][l], p["w_up"][l], p["w_down"][l])
    return _rms_norm(x, p["final_norm"]).reshape(bsz, seqlen, D_MODEL)


def _prepare_params(norm_mix, w_in, conv_w, conv_b, ssd_dt_bias, ssd_a_log, ssd_d, ssd_norm,
                    s5_a_re, s5_a_im, s5_log_dt, s5_b_re, s5_b_im, s5_c_re, s5_c_im, s5_d, s5_glu_w, s5_glu_b,
                    na_rpb, w_out, norm_ffn, router_w, w_gate, w_up, w_down, final_norm):
    bf = jnp.bfloat16
    o0 = D_SSD + D_XBC
    w_in_main = jnp.concatenate([w_in[..., :o0], w_in[..., o0 + 2 * SSD_HEADS:]], axis=-1).astype(bf)
    w_in_dt = jnp.pad(w_in[..., o0:o0 + 2 * SSD_HEADS], ((0, 0), (0, 0), (0, 128 - 2 * SSD_HEADS))).astype(bf)
    s5_ops = [_s5_operators(s5_a_re[l], s5_a_im[l], s5_log_dt[l], s5_b_re[l], s5_b_im[l], s5_c_re[l], s5_c_im[l],
                            s5_d[l]) for l in range(DEPTH)]
    na_bias = [_na_bias_table(na_rpb[l]) for l in range(DEPTH)]
    ssd = [_ssd_params(conv_w[l], conv_b[l], ssd_dt_bias[l], ssd_a_log[l], ssd_d[l], ssd_norm[l]) for l in range(DEPTH)]
    return dict(norm_mix=norm_mix, w_in_main=w_in_main, w_in_dt=w_in_dt, ssd=ssd, s5_ops=s5_ops,
                s5_glu_w=s5_glu_w, s5_glu_b=s5_glu_b, na_bias=na_bias, w_out=w_out.astype(bf), norm_ffn=norm_ffn,
                router_w=router_w, w_gate=w_gate.astype(bf), w_up=w_up.astype(bf), w_down=w_down.astype(bf),
                final_norm=final_norm)


def kernel(x_prompt, x_sample, norm_mix, w_in, conv_w, conv_b, ssd_dt_bias, ssd_a_log, ssd_d, ssd_norm,
           s5_a_re, s5_a_im, s5_log_dt, s5_b_re, s5_b_im, s5_c_re, s5_c_im, s5_d, s5_glu_w, s5_glu_b,
           na_rpb, w_out, norm_ffn, router_w, w_gate, w_up, w_down, final_norm):
    p = _prepare_params(norm_mix, w_in, conv_w, conv_b, ssd_dt_bias, ssd_a_log, ssd_d, ssd_norm,
                        s5_a_re, s5_a_im, s5_log_dt, s5_b_re, s5_b_im, s5_c_re, s5_c_im, s5_d, s5_glu_w, s5_glu_b,
                        na_rpb, w_out, norm_ffn, router_w, w_gate, w_up, w_down, final_norm)
    return (_trunk(x_prompt, p), _trunk(x_sample, p))
```
